```python
import jax
import jax.numpy as jnp
from jax import lax
import numpy as np

D_MODEL = 2048
BATCH = 2
SEQ = 4096
DEPTH = 1

GRID_W = 64
CTX_LEN = 256
D_CONV = D_MODEL // 2
D_REC = D_MODEL // 2
REC_DK = 128
REC_HEADS = D_REC // REC_DK
REC_DV = D_REC // REC_HEADS
REC_CHUNK = 64
CONV_K = 31
D_FF = 4 * D_MODEL
D_IN = 2 * D_CONV + 5 * D_REC
NORM_EPS = 1e-6

kernel_name = "hybrid_conformer_hgrn2_prefix_dit_block"


def rms_norm(x, w):
    xf = x.astype(jnp.float32)
    y = xf * lax.rsqrt(jnp.mean(xf * xf, axis=-1, keepdims=True) + NORM_EPS)
    return (y * w.astype(jnp.float32)).astype(x.dtype)


def layer_norm(x, w, b):
    xf = x.astype(jnp.float32)
    mu = jnp.mean(xf, axis=-1, keepdims=True)
    var = jnp.mean(jnp.square(xf - mu), axis=-1, keepdims=True)
    y = (xf - mu) * lax.rsqrt(var + NORM_EPS)
    return (y * w.astype(jnp.float32) + b.astype(jnp.float32)).astype(x.dtype)


def modulate(h, shift, scale):
    return h * (1 + scale) + shift


def heads(t):
    return t.reshape(t.shape[:-1] + (REC_HEADS, -1))


def flip(t):
    return jnp.flip(t, axis=1)


def depthwise_conv(u, w, b):
    pad = CONV_K // 2
    y = lax.conv_general_dilated(
        u, w[:, None, :].astype(u.dtype), window_strides=(1,), padding=[(pad, pad)],
        dimension_numbers=("NWC", "WIO", "NWC"), feature_group_count=u.shape[-1])
    return y + b.astype(u.dtype)


def conformer_branch(zc, conv_w, conv_b, ln_w, ln_b):
    a, gt = jnp.split(zc, 2, axis=-1)
    u = a * jax.nn.sigmoid(gt)
    u = depthwise_conv(u, conv_w, conv_b)
    return jax.nn.silu(layer_norm(u, ln_w, ln_b))


def rec_gate(fx, lb):
    f = lb + (1 - lb) * jax.nn.sigmoid(fx.astype(jnp.float32))
    return heads(1 - f), heads(jnp.log(f))


def gla_chunked(q, k, v, logf, s0):
    bsz, seq, nh, _ = k.shape
    n = seq // REC_CHUNK

    def blocks(t):
        return t.astype(jnp.float32).reshape(bsz, n, REC_CHUNK, nh, t.shape[-1]).transpose(1, 0, 3, 2, 4)

    kc, vc, gc = blocks(k), blocks(v), blocks(logf)
    b = jnp.cumsum(gc, axis=3)
    b_last = b[:, :, :, -1:, :]
    kv = jnp.einsum("nbhcd,nbhce->nbhde", kc * jnp.exp(b_last - b), vc)
    decay = jnp.exp(b_last[:, :, :, 0, :])

    def step(s, inp):
        dcy, kv_n = inp
        return dcy[..., None] * s + kv_n, s

    s_final, s_in = lax.scan(step, s0.astype(jnp.float32), (decay, kv))
    if q is None:
        return None, s_final
    qc = blocks(q)
    ref = REC_CHUNK // 2 - 1
    b_ref = b[:, :, :, ref:ref + 1, :]
    scores = jnp.einsum("nbhtd,nbhsd->nbhts", qc * jnp.exp(b - b_ref), kc * jnp.exp(b_ref - b))
    mask = jnp.tril(jnp.ones((REC_CHUNK, REC_CHUNK), dtype=bool))
    scores = jnp.where(mask, scores, 0.0)
    o = (jnp.einsum("nbhts,nbhse->nbhte", scores, vc)
         + jnp.einsum("nbhtd,nbhde->nbhte", qc * jnp.exp(b), s_in))
    o = o.transpose(1, 0, 3, 2, 4).reshape(bsz, seq, nh, v.shape[-1])
    return o, s_final


def bidir_hgrn2(q, i, ff, fb, lb, s0_f, s0_b):
    k_f, g_f = rec_gate(ff, lb[0])
    k_b, g_b = rec_gate(fb, lb[1])
    v = heads(i)
    qh = None if q is None else heads(jax.nn.silu(q))
    o_f, s_f = gla_chunked(qh, k_f, v, g_f, s0_f)
    o_b, s_b = gla_chunked(None if qh is None else flip(qh), flip(k_b), flip(v), flip(g_b), s0_b)
    if q is None:
        return None, s_f, s_b
    return o_f + flip(o_b), s_f, s_b


def mixer(h, w_in, w_out, conv_w, conv_b, ln_w, ln_b, lb, rec_norm_w, s0_f, s0_b):
    z = h @ w_in
    zc = z[..., :2 * D_CONV]
    q, g, i, ff, fb = jnp.split(z[..., 2 * D_CONV:], 5, axis=-1)
    u = conformer_branch(zc, conv_w, conv_b, ln_w, ln_b)
    o, s_f, s_b = bidir_hgrn2(q, i, ff, fb, lb, s0_f, s0_b)
    o = rms_norm(o, rec_norm_w.reshape(REC_HEADS, REC_DV)).astype(h.dtype)
    o = o.reshape(h.shape[:-1] + (D_REC,)) * jax.nn.silu(g)
    y = jnp.concatenate([u, o], axis=-1) @ w_out
    return y, s_f, s_b


def context_rec_states(h_ctx, w_in, lb, s0):
    z = h_ctx @ w_in[:, 2 * D_CONV + 2 * D_REC:]
    i, ff, fb = jnp.split(z, 3, axis=-1)
    _, s_f, s_b = bidir_hgrn2(None, i, ff, fb, lb, s0, s0)
    return s_f, s_b


def sq_relu_mlp(h, w_up, w_down):
    return jnp.square(jax.nn.relu(h @ w_up)) @ w_down


def setup_inputs(seed: int = 0) -> dict:
    key = jax.random.key(seed)
    ks = jax.random.split(key, 20)
    D = D_MODEL

    def nrm(k, shape, s):
        return jax.random.normal(k, shape, jnp.float32) * s

    return {
        "x": nrm(ks[0], (BATCH, SEQ, D), 1.0),
        "c": nrm(ks[1], (BATCH, D), 1.0),
        "ctx": nrm(ks[2], (BATCH, CTX_LEN, D), 1.0),
        "c_ctx": nrm(ks[3], (D,), 1.0),
        "w_ada": nrm(ks[4], (DEPTH, D, 6 * D), D ** -0.5),
        "b_ada": nrm(ks[5], (DEPTH, 6 * D), 0.02),
        "mix_pre_w": 1.0 + nrm(ks[6], (DEPTH, D), 0.02),
        "mix_post_w": 1.0 + nrm(ks[7], (DEPTH, D), 0.02),
        "mlp_pre_w": 1.0 + nrm(ks[8], (DEPTH, D), 0.02),
        "mlp_post_w": 1.0 + nrm(ks[9], (DEPTH, D), 0.02),
        "w_in": nrm(ks[10], (DEPTH, D, D_IN), D ** -0.5),
        "conv_w": nrm(ks[11], (DEPTH, CONV_K, D_CONV), CONV_K ** -0.5),
        "conv_b": nrm(ks[12], (DEPTH, D_CONV), 0.02),
        "conv_ln_w": 1.0 + nrm(ks[13], (DEPTH, D_CONV), 0.02),
        "conv_ln_b": nrm(ks[14], (DEPTH, D_CONV), 0.02),
        "rec_lb_logits": nrm(ks[15], (DEPTH + 1, 2, D_REC), 0.1),
        "rec_norm_w": 1.0 + nrm(ks[16], (DEPTH, D_REC), 0.02),
        "w_out": nrm(ks[17], (DEPTH, D_CONV + D_REC, D), (D_CONV + D_REC) ** -0.5),
        "w_up": nrm(ks[18], (DEPTH, D, D_FF), D ** -0.5),
        "w_down": nrm(ks[19], (DEPTH, D_FF, D), D_FF ** -0.5),
    }


def reference(x, c, ctx, c_ctx, w_ada, b_ada, mix_pre_w, mix_post_w, mlp_pre_w, mlp_post_w,
              w_in, conv_w, conv_b, conv_ln_w, conv_ln_b, rec_lb_logits, rec_norm_w,
              w_out, w_up, w_down):
    bsz = x.shape[0]
    lb_all = jnp.cumsum(jax.nn.softmax(rec_lb_logits.astype(jnp.float32), axis=0), axis=0)
    s_zero = jnp.zeros((bsz, REC_HEADS, REC_DK, REC_DV), jnp.float32)
    for l in range(DEPTH):
        mod_x = (jax.nn.silu(c) @ w_ada[l] + b_ada[l])[:, None, :]
        mod_c = jax.nn.silu(c_ctx) @ w_ada[l] + b_ada[l]
        sh1x, sc1x, g1x, sh2x, sc2x, g2x = jnp.split(mod_x, 6, axis=-1)
        sh1c, sc1c, g1c, sh2c, sc2c, g2c = jnp.split(mod_c, 6, axis=-1)
        lb = lb_all[l]

        hc = modulate(rms_norm(ctx, mix_pre_w[l]), sh1c, sc1c)
        if l < DEPTH - 1:
            yc, s_f, s_b = mixer(hc, w_in[l], w_out[l], conv_w[l], conv_b[l], conv_ln_w[l], conv_ln_b[l],
                                 lb, rec_norm_w[l], s_zero, s_zero)
            ctx = ctx + g1c * rms_norm(yc, mix_post_w[l])
            hc2 = modulate(rms_norm(ctx, mlp_pre_w[l]), sh2c, sc2c)
            ctx = ctx + g2c * rms_norm(sq_relu_mlp(hc2, w_up[l], w_down[l]), mlp_post_w[l])
        else:
            s_f, s_b = context_rec_states(hc, w_in[l], lb, s_zero)

        hx = modulate(rms_norm(x, mix_pre_w[l]), sh1x, sc1x)
        yx, _, _ = mixer(hx, w_in[l], w_out[l], conv_w[l], conv_b[l], conv_ln_w[l], conv_ln_b[l],
                         lb, rec_norm_w[l], s_f, s_b)
        x = x + g1x * rms_norm(yx, mix_post_w[l])
        hx2 = modulate(rms_norm(x, mlp_pre_w[l]), sh2x, sc2x)
        x = x + g2x * rms_norm(sq_relu_mlp(hx2, w_up[l], w_down[l]), mlp_post_w[l])
    return x
```

```python
import functools

import jax
import jax.numpy as jnp
from jax import lax
from jax.experimental import pallas as pl
from jax.experimental.pallas import tpu as pltpu

NORM_EPS = 1e-6
REC_DK = 128
REC_CHUNK = 64
CONV_K = 31
CONV_HALO = 16
SUBLANES = 8
VMEM_LIMIT_BYTES = 56 * 1024 * 1024

F32 = jnp.float32
BF16 = jnp.bfloat16


def _params(n_axes):
    return pltpu.CompilerParams(dimension_semantics=("arbitrary",) * n_axes,
                                vmem_limit_bytes=VMEM_LIMIT_BYTES)


def _sigmoid(x):
    return 1.0 / (1.0 + jnp.exp(-x))


def _rms_norm(xf, w):
    ms = jnp.mean(xf * xf, axis=-1, keepdims=True)
    return xf * lax.rsqrt(ms + NORM_EPS) * w


def _dot(a, b):
    return jnp.dot(a, b, preferred_element_type=F32)


def _dot_nt(a, b):
    return lax.dot_general(a, b, (((1,), (1,)), ((), ())), preferred_element_type=F32)


def _dot_tn(a, b):
    return lax.dot_general(a, b, (((0,), (0,)), ((), ())), preferred_element_type=F32)


def _tri_sum(tri, g):
    g_hi = g.astype(BF16)
    g_lo = (g - g_hi.astype(F32)).astype(BF16)
    return _dot(tri, g_hi) + _dot(tri, g_lo)


def _lower_bound(lbl, direction):
    m = jnp.max(lbl, axis=0)
    e = jnp.exp(lbl - m[None])
    sm0 = e[0] / jnp.sum(e, axis=0)
    return sm0[direction:direction + 1]


def _log_forget(z, lb):
    return jnp.log(lb + (1.0 - lb) * _sigmoid(z))


def _ada_kernel(c_ref, w_ref, b_ref, o_ref):
    c = c_ref[...]
    s = (c * _sigmoid(c)).astype(BF16)
    o_ref[...] = _dot(s, w_ref[...].astype(BF16)) + b_ref[...]


def _ada(cc, w_ada, b_ada, tn):
    rows, d = cc.shape
    n = w_ada.shape[1]
    return pl.pallas_call(
        _ada_kernel,
        grid=(n // tn,),
        in_specs=[pl.BlockSpec((rows, d), lambda j: (0, 0)),
                  pl.BlockSpec((d, tn), lambda j: (0, j)),
                  pl.BlockSpec((1, tn), lambda j: (0, j))],
        out_specs=pl.BlockSpec((rows, tn), lambda j: (0, j)),
        out_shape=jax.ShapeDtypeStruct((rows, n), F32),
        compiler_params=_params(1),
        name="ada",
    )(cc, w_ada, b_ada)


def _ctx_kernel(ctx_ref, mod_ref, npw_ref, lbl_ref, wi_ref, wf_ref, wb_ref, sf_ref, sb_ref, h_scr):
    @pl.when(pl.program_id(1) == 0)
    def _():
        h = _rms_norm(ctx_ref[0], npw_ref[...])
        h_scr[...] = (h * (1.0 + mod_ref[0, 1:2, :]) + mod_ref[0, 0:1, :]).astype(BF16)

    hb = h_scr[...]
    lc = hb.shape[0]
    v = _dot(hb, wi_ref[...]).astype(BF16)
    lbl = lbl_ref[...]
    g_f = _log_forget(_dot(hb, wf_ref[...]), _lower_bound(lbl, 0))
    g_b = _log_forget(_dot(hb, wb_ref[...]), _lower_bound(lbl, 1))
    row = lax.broadcasted_iota(jnp.int32, (lc, lc), 0)
    col = lax.broadcasted_iota(jnp.int32, (lc, lc), 1)
    incl = (col <= row).astype(BF16)
    excl = (col < row).astype(BF16)
    b_f = _tri_sum(incl, g_f)
    kd_f = (1.0 - jnp.exp(g_f)) * jnp.exp(b_f[lc - 1:lc, :] - b_f)
    c_b = _tri_sum(excl, g_b)
    kd_b = (1.0 - jnp.exp(g_b)) * jnp.exp(c_b)
    sf_ref[0, 0] = _dot_tn(v, kd_f.astype(BF16))
    sb_ref[0, 0] = _dot_tn(v, kd_b.astype(BF16))


def _ctx_states(ctx, mod3, npw, lbl, w_in_bf, col0):
    bsz, lc, d = ctx.shape
    nh = lbl.shape[-1] // REC_DK
    cb = col0 // REC_DK
    wspec = lambda off: pl.BlockSpec((d, REC_DK), lambda b, h, off=off: (0, cb + off * nh + h))
    st_spec = pl.BlockSpec((1, 1, REC_DK, REC_DK), lambda b, h: (b, h, 0, 0))
    st_shape = jax.ShapeDtypeStruct((bsz, nh, REC_DK, REC_DK), F32)
    return pl.pallas_call(
        _ctx_kernel,
        grid=(bsz, nh),
        in_specs=[pl.BlockSpec((1, lc, d), lambda b, h: (b, 0, 0)),
                  pl.BlockSpec((1,) + mod3.shape[1:], lambda b, h: (bsz, 0, 0)),
                  pl.BlockSpec((1, d), lambda b, h: (0, 0)),
                  pl.BlockSpec(lbl.shape[:2] + (REC_DK,), lambda b, h: (0, 0, h)),
                  wspec(0), wspec(1), wspec(2)],
        out_specs=[st_spec, st_spec],
        out_shape=[st_shape, st_shape],
        scratch_shapes=[pltpu.VMEM((lc, d), BF16)],
        compiler_params=_params(2),
        name="ctx_states",
    )(ctx, mod3, npw, lbl, w_in_bf, w_in_bf, w_in_bf)


def _inproj_kernel(x_ref, mod_ref, npw_ref, lbl_ref, w_ref, o_ref, h_scr, a_scr):
    j = pl.program_id(1)

    @pl.when(j == 0)
    def _():
        h = _rms_norm(x_ref[...], npw_ref[...])
        h_scr[...] = (h * (1.0 + mod_ref[0, 1:2, :]) + mod_ref[0, 0:1, :]).astype(BF16)

    z = _dot(h_scr[...], w_ref[...])

    @pl.when(j == 0)
    def _():
        a_scr[...] = z

    @pl.when(j == 1)
    def _():
        o_ref[...] = a_scr[...] * _sigmoid(z)

    @pl.when((j == 2) | (j == 3))
    def _():
        o_ref[...] = z * _sigmoid(z)

    @pl.when(j == 4)
    def _():
        o_ref[...] = z

    @pl.when(j == 5)
    def _():
        o_ref[...] = _log_forget(z, _lower_bound(lbl_ref[...], 0))

    @pl.when(j == 6)
    def _():
        o_ref[...] = _log_forget(z, _lower_bound(lbl_ref[...], 1))


def _inproj(x2, mod3, npw, lbl, w_in_bf, tm, rows_per_batch):
    m, d = x2.shape
    gw = d // 2
    n_groups = w_in_bf.shape[1] // gw
    tiles_per_batch = rows_per_batch // tm
    return pl.pallas_call(
        _inproj_kernel,
        grid=(m // tm, n_groups),
        in_specs=[pl.BlockSpec((tm, d), lambda i, j: (i, 0)),
                  pl.BlockSpec((1,) + mod3.shape[1:], lambda i, j: (i // tiles_per_batch, 0, 0)),
                  pl.BlockSpec((1, d), lambda i, j: (0, 0)),
                  pl.BlockSpec(lbl.shape, lambda i, j: (0, 0, 0)),
                  pl.BlockSpec((d, gw), lambda i, j: (0, j))],
        out_specs=pl.BlockSpec((tm, gw), lambda i, j: (i, jnp.maximum(j - 1, 0))),
        out_shape=jax.ShapeDtypeStruct((m, (n_groups - 1) * gw), F32),
        scratch_shapes=[pltpu.VMEM((tm, d), BF16), pltpu.VMEM((tm, gw), F32)],
        compiler_params=_params(2),
        name="inproj",
    )(x2, mod3, npw, lbl, w_in_bf)


def _conv_kernel(u_ref, ul_ref, ur_ref, cw_ref, cb_ref, lnw_ref, lnb_ref, o_ref, buf, y_scr, *, rb):
    t = pl.program_id(1)
    tl, ch = u_ref.shape[1], u_ref.shape[2]
    buf[pl.ds(CONV_HALO, tl), :] = u_ref[0]
    buf[pl.ds(0, CONV_HALO), :] = jnp.where(t > 0, ul_ref[0], 0.0)
    buf[pl.ds(CONV_HALO + tl, CONV_HALO), :] = jnp.where(t < pl.num_programs(1) - 1, ur_ref[0], 0.0)

    shift0 = CONV_HALO - CONV_K // 2
    win = rb + 2 * CONV_HALO

    def row_block(i, carry):
        r0 = pl.multiple_of(i * rb, rb)
        for c0 in range(0, ch, 128):
            w = buf[pl.ds(r0, win), c0:c0 + 128]
            acc = jnp.zeros((rb, 128), F32)
            for phase in range(SUBLANES):
                taps = [k for k in range(CONV_K) if (k + shift0) % SUBLANES == phase]
                if not taps:
                    continue
                wp = w[phase:phase + win - SUBLANES, :]
                for k in taps:
                    a0 = (k + shift0) // SUBLANES * SUBLANES
                    acc = acc + wp[a0:a0 + rb, :] * cw_ref[k:k + 1, c0:c0 + 128]
            y_scr[pl.ds(r0, rb), c0:c0 + 128] = acc
        return carry

    lax.fori_loop(0, tl // rb, row_block, 0)

    y = y_scr[...] + cb_ref[...]
    mu = jnp.mean(y, axis=-1, keepdims=True)
    yc = y - mu
    var = jnp.mean(yc * yc, axis=-1, keepdims=True)
    yn = yc * lax.rsqrt(var + NORM_EPS) * lnw_ref[...] + lnb_ref[...]
    o_ref[0] = yn * _sigmoid(yn)


def _conv(z3, conv_w, conv_b, ln_w, ln_b, tl, rb):
    bsz, seq, _ = z3.shape
    ch = conv_w.shape[1]
    hb = tl // CONV_HALO
    n_halo = seq // CONV_HALO
    vec = pl.BlockSpec((1, ch), lambda b, t: (0, 0))
    return pl.pallas_call(
        functools.partial(_conv_kernel, rb=rb),
        grid=(bsz, seq // tl),
        in_specs=[pl.BlockSpec((1, tl, ch), lambda b, t: (b, t, 0)),
                  pl.BlockSpec((1, CONV_HALO, ch), lambda b, t: (b, jnp.maximum(t * hb - 1, 0), 0)),
                  pl.BlockSpec((1, CONV_HALO, ch), lambda b, t: (b, jnp.minimum((t + 1) * hb, n_halo - 1), 0)),
                  pl.BlockSpec((CONV_K, ch), lambda b, t: (0, 0)),
                  vec, vec, vec],
        out_specs=pl.BlockSpec((1, tl, ch), lambda b, t: (b, t, 0)),
        out_shape=jax.ShapeDtypeStruct((bsz, seq, ch), F32),
        scratch_shapes=[pltpu.VMEM((tl + 2 * CONV_HALO, ch), F32), pltpu.VMEM((tl, ch), F32)],
        compiler_params=_params(2),
        name="conv",
    )(z3, z3, z3, conv_w, conv_b, ln_w, ln_b)


def _gla_chunk(q, v, g, st, tri, mask, last, ref):
    b = _tri_sum(tri, g)
    b_last = b[last:last + 1, :]
    b_ref = b[ref:ref + 1, :]
    k = 1.0 - jnp.exp(g)
    vb = v.astype(BF16)
    kd = (k * jnp.exp(b_last - b)).astype(BF16)
    qa = (q * jnp.exp(b - b_ref)).astype(BF16)
    ka = (k * jnp.exp(b_ref - b)).astype(BF16)
    scores = jnp.where(mask, _dot_nt(qa, ka), 0.0)
    o = _dot(scores.astype(BF16), vb) + _dot_nt((q * jnp.exp(b)).astype(BF16), st.astype(BF16))
    st_new = jnp.exp(b_last) * st + _dot_tn(vb, kd)
    return o, st_new


def _gla_kernel(qf_ref, vf_ref, gf_ref, qb_ref, vb_ref, gb_ref, s0f_ref, s0b_ref,
                of_ref, ob_ref, stf, stb):
    @pl.when(pl.program_id(1) == 0)
    def _():
        stf[...] = s0f_ref[0]
        stb[...] = s0b_ref[0]

    rows, width = qf_ref.shape[1], qf_ref.shape[2]
    n_chunks = rows // REC_CHUNK
    r_i = lax.broadcasted_iota(jnp.int32, (REC_CHUNK, REC_CHUNK), 0)
    c_i = lax.broadcasted_iota(jnp.int32, (REC_CHUNK, REC_CHUNK), 1)
    lower = c_i <= r_i
    upper = c_i >= r_i
    tri_l = lower.astype(BF16)
    tri_u = upper.astype(BF16)
    ref_f = REC_CHUNK // 2 - 1
    ref_b = REC_CHUNK - 1 - ref_f

    def body(c, carry):
        rf = pl.multiple_of(c * REC_CHUNK, REC_CHUNK)
        rb = pl.multiple_of((n_chunks - 1 - c) * REC_CHUNK, REC_CHUNK)
        for h in range(width // REC_DK):
            sl = slice(h * REC_DK, (h + 1) * REC_DK)
            o, st = _gla_chunk(qf_ref[0, pl.ds(rf, REC_CHUNK), sl], vf_ref[0, pl.ds(rf, REC_CHUNK), sl],
                               gf_ref[0, pl.ds(rf, REC_CHUNK), sl], stf[h], tri_l, lower, REC_CHUNK - 1, ref_f)
            of_ref[0, pl.ds(rf, REC_CHUNK), sl] = o
            stf[h] = st
            o, st = _gla_chunk(qb_ref[0, pl.ds(rb, REC_CHUNK), sl], vb_ref[0, pl.ds(rb, REC_CHUNK), sl],
                               gb_ref[0, pl.ds(rb, REC_CHUNK), sl], stb[h], tri_u, upper, 0, ref_b)
            ob_ref[0, pl.ds(rb, REC_CHUNK), sl] = o
            stb[h] = st
        return carry

    lax.fori_loop(0, n_chunks, body, 0)


def _gla(z3, s0f, s0b, rows, width):
    bsz, seq, _ = z3.shape
    nb = seq // rows
    nh = width // REC_DK
    fwd = lambda col: pl.BlockSpec((1, rows, width), lambda b, n, col=col: (b, n, col))
    bwd = lambda col: pl.BlockSpec((1, rows, width), lambda b, n, col=col: (b, nb - 1 - n, col))
    st_spec = pl.BlockSpec((1, nh, REC_DK, REC_DK), lambda b, n: (b, 0, 0, 0))
    o_shape = jax.ShapeDtypeStruct((bsz, seq, width), F32)
    return pl.pallas_call(
        _gla_kernel,
        grid=(bsz, nb),
        in_specs=[fwd(1), fwd(3), fwd(4), bwd(1), bwd(3), bwd(5), st_spec, st_spec],
        out_specs=[pl.BlockSpec((1, rows, width), lambda b, n: (b, n, 0)),
                   pl.BlockSpec((1, rows, width), lambda b, n: (b, nb - 1 - n, 0))],
        out_shape=[o_shape, o_shape],
        scratch_shapes=[pltpu.VMEM((nh, REC_DK, REC_DK), F32), pltpu.VMEM((nh, REC_DK, REC_DK), F32)],
        compiler_params=_params(2),
        name="gla",
    )(z3, z3, z3, z3, z3, z3, s0f, s0b)


def _outproj_kernel(u_ref, of_ref, ob_ref, gs_ref, x_ref, mod_ref, rnw_ref, postw_ref, prew_ref, w_ref,
                    x1_ref, h2_ref):
    gw = u_ref.shape[1]
    o = of_ref[...] + ob_ref[...]
    heads = []
    for h in range(gw // REC_DK):
        oh = o[:, h * REC_DK:(h + 1) * REC_DK]
        heads.append(oh * lax.rsqrt(jnp.mean(oh * oh, axis=-1, keepdims=True) + NORM_EPS))
    on = jnp.concatenate(heads, axis=-1) * rnw_ref[...] * gs_ref[...]
    y = _dot(u_ref[...].astype(BF16), w_ref[0:gw, :]) + _dot(on.astype(BF16), w_ref[gw:2 * gw, :])
    x1 = x_ref[...] + mod_ref[0, 2:3, :] * _rms_norm(y, postw_ref[...])
    x1_ref[...] = x1
    h2 = _rms_norm(x1, prew_ref[...])
    h2_ref[...] = (h2 * (1.0 + mod_ref[0, 4:5, :]) + mod_ref[0, 3:4, :]).astype(BF16)


def _outproj(uconv, o_f, o_b, z2, x2, mod3, rnw, postw, prew, w_out_bf, tm, rows_per_batch):
    m, d = x2.shape
    gw = d // 2
    tiles_per_batch = rows_per_batch // tm
    half = pl.BlockSpec((tm, gw), lambda i: (i, 0))
    full = pl.BlockSpec((tm, d), lambda i: (i, 0))
    vec = lambda n: pl.BlockSpec((1, n), lambda i: (0, 0))
    return pl.pallas_call(
        _outproj_kernel,
        grid=(m // tm,),
        in_specs=[half, half, half,
                  pl.BlockSpec((tm, gw), lambda i: (i, 2)),
                  full,
                  pl.BlockSpec((1,) + mod3.shape[1:], lambda i: (i // tiles_per_batch, 0, 0)),
                  vec(gw), vec(d), vec(d),
                  pl.BlockSpec((d, d), lambda i: (0, 0))],
        out_specs=[full, full],
        out_shape=[jax.ShapeDtypeStruct((m, d), F32), jax.ShapeDtypeStruct((m, d), BF16)],
        compiler_params=_params(1),
        name="outproj",
    )(uconv, o_f, o_b, z2, x2, mod3, rnw, postw, prew, w_out_bf)


def _mlp_kernel(h_ref, x1_ref, mod_ref, postw_ref, wu_ref, wd_ref, o_ref, acc):
    k = pl.program_id(1)

    @pl.when(k == 0)
    def _():
        acc[...] = jnp.zeros_like(acc)

    hid = jnp.maximum(_dot(h_ref[...], wu_ref[...]), 0.0)
    acc[...] += _dot((hid * hid).astype(BF16), wd_ref[...])

    @pl.when(k == pl.num_programs(1) - 1)
    def _():
        o_ref[...] = x1_ref[...] + mod_ref[0, 5:6, :] * _rms_norm(acc[...], postw_ref[...])


def _mlp(h2, x1, mod3, postw, w_up_bf, w_down_bf, tm, tf, rows_per_batch):
    m, d = x1.shape
    ff = w_up_bf.shape[1]
    tiles_per_batch = rows_per_batch // tm
    return pl.pallas_call(
        _mlp_kernel,
        grid=(m // tm, ff // tf),
        in_specs=[pl.BlockSpec((tm, d), lambda i, k: (i, 0)),
                  pl.BlockSpec((tm, d), lambda i, k: (i, 0)),
                  pl.BlockSpec((1,) + mod3.shape[1:], lambda i, k: (i // tiles_per_batch, 0, 0)),
                  pl.BlockSpec((1, d), lambda i, k: (0, 0)),
                  pl.BlockSpec((d, tf), lambda i, k: (0, k)),
                  pl.BlockSpec((tf, d), lambda i, k: (k, 0))],
        out_specs=pl.BlockSpec((tm, d), lambda i, k: (i, 0)),
        out_shape=jax.ShapeDtypeStruct((m, d), F32),
        scratch_shapes=[pltpu.VMEM((tm, d), F32)],
        compiler_params=_params(2),
        name="mlp",
    )(h2, x1, mod3, postw, w_up_bf, w_down_bf)


def _tile(n, target):
    t = min(n, target)
    assert n % t == 0, (n, t)
    return t


def kernel(x, c, ctx, c_ctx, w_ada, b_ada, mix_pre_w, mix_post_w, mlp_pre_w, mlp_post_w, w_in, conv_w, conv_b,
           conv_ln_w, conv_ln_b, rec_lb_logits, rec_norm_w, w_out, w_up, w_down):
    bsz, seq, d = x.shape
    assert w_in.shape[0] == 1, "single-layer block"
    gw = d // 2
    assert gw % REC_DK == 0 and seq % REC_CHUNK == 0 and ctx.shape[1] % SUBLANES == 0
    m = bsz * seq

    cc = jnp.concatenate([c, c_ctx[None, :], jnp.zeros((SUBLANES - bsz - 1, d), c.dtype)], axis=0)
    mod = _ada(cc, w_ada[0], b_ada, _tile(w_ada.shape[2], 1024))
    mod3 = mod.reshape(SUBLANES, 6, d)

    w_in_bf = w_in[0].astype(BF16)
    w_out_bf = w_out[0].astype(BF16)
    w_up_bf = w_up[0].astype(BF16)
    w_down_bf = w_down[0].astype(BF16)
    lbl = rec_lb_logits.astype(F32)

    s0f, s0b = _ctx_states(ctx, mod3, mix_pre_w, lbl, w_in_bf, col0=2 * gw + 2 * gw)

    x2 = x.reshape(m, d)
    tm = _tile(seq, 512)
    z2 = _inproj(x2, mod3, mix_pre_w, lbl, w_in_bf, tm, seq)
    z3 = z2.reshape(bsz, seq, 6 * gw)

    tl = _tile(seq, 512)
    uconv = _conv(z3, conv_w[0], conv_b, conv_ln_w, conv_ln_b, tl, _tile(tl, 32))
    o_f, o_b = _gla(z3, s0f, s0b, _tile(seq, 256), gw)

    x1, h2 = _outproj(uconv.reshape(m, gw), o_f.reshape(m, gw), o_b.reshape(m, gw), z2, x2, mod3,
                      rec_norm_w, mix_post_w, mlp_pre_w, w_out_bf, _tile(seq, 256), seq)
    out = _mlp(h2, x1, mod3, mlp_post_w, w_up_bf, w_down_bf, tm, _tile(w_up.shape[2], 1024), seq)
    return out.reshape(bsz, seq, d)
```

```python
import functools

import jax
import jax.numpy as jnp
from jax import lax
from jax.experimental import pallas as pl
from jax.experimental.pallas import tpu as pltpu

NORM_EPS = 1e-6
REC_DK = 128
REC_CHUNK = 64
CONV_K = 31
CONV_HALO = 16
SUBLANES = 8
EPILOGUE_ROWS = 32
VMEM_LIMIT_BYTES = 56 * 1024 * 1024

F32 = jnp.float32
BF16 = jnp.bfloat16


def _params(n_axes):
    return pltpu.CompilerParams(dimension_semantics=("arbitrary",) * n_axes,
                                vmem_limit_bytes=VMEM_LIMIT_BYTES)


def _sigmoid(x):
    return 1.0 / (1.0 + jnp.exp(-x))


def _rms_norm(xf, w):
    ms = jnp.mean(xf * xf, axis=-1, keepdims=True)
    return xf * lax.rsqrt(ms + NORM_EPS) * w


def _dot(a, b):
    return jnp.dot(a, b, preferred_element_type=F32)


def _dot_nt(a, b):
    return lax.dot_general(a, b, (((1,), (1,)), ((), ())), preferred_element_type=F32)


def _dot_tn(a, b):
    return lax.dot_general(a, b, (((0,), (0,)), ((), ())), preferred_element_type=F32)


def _tri_sum(tri, g):
    if g.dtype == BF16:
        return _dot(tri, g)
    g_hi = g.astype(BF16)
    g_lo = (g - g_hi.astype(F32)).astype(BF16)
    return _dot(tri, g_hi) + _dot(tri, g_lo)


def _lower_bound(lbl, direction):
    m = jnp.max(lbl, axis=0)
    e = jnp.exp(lbl - m[None])
    sm0 = e[0] / jnp.sum(e, axis=0)
    return sm0[direction:direction + 1]


def _log_forget(z, lb):
    return jnp.log(lb + (1.0 - lb) * _sigmoid(z))


def _ada_kernel(c_ref, w_ref, b_ref, o_ref):
    c = c_ref[...]
    s = (c * _sigmoid(c)).astype(BF16)
    o_ref[...] = _dot(s, w_ref[...].astype(BF16)) + b_ref[...]


def _ada(cc, w_ada, b_ada, tn):
    rows, d = cc.shape
    n = w_ada.shape[1]
    return pl.pallas_call(
        _ada_kernel,
        grid=(n // tn,),
        in_specs=[pl.BlockSpec((rows, d), lambda j: (0, 0)),
                  pl.BlockSpec((d, tn), lambda j: (0, j)),
                  pl.BlockSpec((1, tn), lambda j: (0, j))],
        out_specs=pl.BlockSpec((rows, tn), lambda j: (0, j)),
        out_shape=jax.ShapeDtypeStruct((rows, n), F32),
        compiler_params=_params(1),
        name="ada",
    )(cc, w_ada, b_ada)


def _ctx_kernel(ctx_ref, mod_ref, npw_ref, lbl_ref, wi_ref, wf_ref, wb_ref, sf_ref, sb_ref, h_scr):
    @pl.when(pl.program_id(1) == 0)
    def _():
        h = _rms_norm(ctx_ref[0], npw_ref[...])
        h_scr[...] = (h * (1.0 + mod_ref[0, 1:2, :]) + mod_ref[0, 0:1, :]).astype(BF16)

    hb = h_scr[...]
    lc = hb.shape[0]
    v = _dot(hb, wi_ref[...]).astype(BF16)
    lbl = lbl_ref[...]
    g_f = _log_forget(_dot(hb, wf_ref[...]), _lower_bound(lbl, 0))
    g_b = _log_forget(_dot(hb, wb_ref[...]), _lower_bound(lbl, 1))
    row = lax.broadcasted_iota(jnp.int32, (lc, lc), 0)
    col = lax.broadcasted_iota(jnp.int32, (lc, lc), 1)
    incl = (col <= row).astype(BF16)
    excl = (col < row).astype(BF16)
    b_f = _tri_sum(incl, g_f)
    kd_f = (1.0 - jnp.exp(g_f)) * jnp.exp(b_f[lc - 1:lc, :] - b_f)
    c_b = _tri_sum(excl, g_b)
    kd_b = (1.0 - jnp.exp(g_b)) * jnp.exp(c_b)
    sf_ref[0, 0] = _dot_tn(v, kd_f.astype(BF16))
    sb_ref[0, 0] = _dot_tn(v, kd_b.astype(BF16))


def _ctx_states(ctx, mod3, npw, lbl, w_in_bf, col0):
    bsz, lc, d = ctx.shape
    nh = lbl.shape[-1] // REC_DK
    cb = col0 // REC_DK
    wspec = lambda off: pl.BlockSpec((d, REC_DK), lambda b, h, off=off: (0, cb + off * nh + h))
    st_spec = pl.BlockSpec((1, 1, REC_DK, REC_DK), lambda b, h: (b, h, 0, 0))
    st_shape = jax.ShapeDtypeStruct((bsz, nh, REC_DK, REC_DK), F32)
    return pl.pallas_call(
        _ctx_kernel,
        grid=(bsz, nh),
        in_specs=[pl.BlockSpec((1, lc, d), lambda b, h: (b, 0, 0)),
                  pl.BlockSpec((1,) + mod3.shape[1:], lambda b, h: (bsz, 0, 0)),
                  pl.BlockSpec((1, d), lambda b, h: (0, 0)),
                  pl.BlockSpec(lbl.shape[:2] + (REC_DK,), lambda b, h: (0, 0, h)),
                  wspec(0), wspec(1), wspec(2)],
        out_specs=[st_spec, st_spec],
        out_shape=[st_shape, st_shape],
        scratch_shapes=[pltpu.VMEM((lc, d), BF16)],
        compiler_params=_params(2),
        name="ctx_states",
    )(ctx, mod3, npw, lbl, w_in_bf, w_in_bf, w_in_bf)


def _inproj_kernel(x_ref, mod_ref, npw_ref, lbl_ref, w_ref, o_ref, h_scr, a_scr, *, tn):
    j = pl.program_id(1)

    @pl.when(j == 0)
    def _():
        h = _rms_norm(x_ref[...], npw_ref[...])
        h_scr[...] = (h * (1.0 + mod_ref[0, 1:2, :]) + mod_ref[0, 0:1, :]).astype(BF16)

    @pl.when((j == 0) & (pl.program_id(0) == 0))
    def _():
        a_scr[...] = jnp.zeros_like(a_scr)

    lbl = lbl_ref[...]
    lb = jnp.where(j == 6, _lower_bound(lbl, 1), _lower_bound(lbl, 0))
    for n0 in range(0, w_ref.shape[1], tn):
        cs = slice(n0, n0 + tn)
        z_all = _dot(h_scr[...], w_ref[:, cs])
        for r0 in range(0, z_all.shape[0], EPILOGUE_ROWS):
            rs = slice(r0, r0 + EPILOGUE_ROWS)
            z = z_all[rs]
            sig = _sigmoid(z)
            log_f = jnp.log(lb[:, cs] + (1.0 - lb[:, cs]) * sig)
            out = jnp.where(j == 1, a_scr[rs, cs] * sig,
                            jnp.where(j <= 3, z * sig, jnp.where(j == 4, z, log_f)))
            o_ref[rs, cs] = out.astype(o_ref.dtype)
            a_scr[rs, cs] = z


def _inproj(x2, mod3, npw, lbl, w_in_bf, tm, tn, rows_per_batch):
    m, d = x2.shape
    gw = d // 2
    n_groups = w_in_bf.shape[1] // gw
    tiles_per_batch = rows_per_batch // tm
    return pl.pallas_call(
        functools.partial(_inproj_kernel, tn=tn),
        grid=(m // tm, n_groups),
        in_specs=[pl.BlockSpec((tm, d), lambda i, j: (i, 0)),
                  pl.BlockSpec((1,) + mod3.shape[1:], lambda i, j: (i // tiles_per_batch, 0, 0)),
                  pl.BlockSpec((1, d), lambda i, j: (0, 0)),
                  pl.BlockSpec(lbl.shape, lambda i, j: (0, 0, 0)),
                  pl.BlockSpec((d, gw), lambda i, j: (0, j))],
        out_specs=pl.BlockSpec((tm, gw), lambda i, j: (i, jnp.maximum(j - 1, 0))),
        out_shape=jax.ShapeDtypeStruct((m, (n_groups - 1) * gw), BF16),
        scratch_shapes=[pltpu.VMEM((tm, d), BF16), pltpu.VMEM((tm, gw), F32)],
        compiler_params=_params(2),
        name="inproj",
    )(x2, mod3, npw, lbl, w_in_bf)


def _conv_kernel(u_ref, ul_ref, ur_ref, cw_ref, cb_ref, lnw_ref, lnb_ref, o_ref, buf, sh_scr, y_scr, *, rb):
    t = pl.program_id(1)
    tl, ch = u_ref.shape[1], u_ref.shape[2]
    buf[pl.ds(CONV_HALO, tl), :] = u_ref[0]
    buf[pl.ds(0, CONV_HALO), :] = jnp.where(t > 0, ul_ref[0], jnp.zeros_like(ul_ref[0]))
    buf[pl.ds(CONV_HALO + tl, CONV_HALO), :] = jnp.where(t < pl.num_programs(1) - 1, ur_ref[0],
                                                          jnp.zeros_like(ur_ref[0]))

    shift0 = CONV_HALO - CONV_K // 2
    win = rb + 2 * CONV_HALO
    span = win - SUBLANES
    r_i = lax.broadcasted_iota(jnp.int32, (span, win), 0)
    c_i = lax.broadcasted_iota(jnp.int32, (span, win), 1)
    shifts = [(c_i == r_i + p).astype(BF16) for p in range(SUBLANES)]

    def row_block(i, carry):
        r0 = pl.multiple_of(i * rb, rb)
        w = buf[pl.ds(r0, win), :]
        for p in range(SUBLANES):
            sh_scr[p] = _dot(shifts[p], w)
        for c0 in range(0, ch, 128):
            acc = jnp.zeros((rb, 128), F32)
            for k in range(CONV_K):
                p, a0 = (k + shift0) % SUBLANES, (k + shift0) // SUBLANES * SUBLANES
                acc = acc + sh_scr[p, a0:a0 + rb, c0:c0 + 128] * cw_ref[k:k + 1, c0:c0 + 128]
            y_scr[pl.ds(r0, rb), c0:c0 + 128] = acc
        return carry

    lax.fori_loop(0, tl // rb, row_block, 0)

    y = y_scr[...] + cb_ref[...]
    mu = jnp.mean(y, axis=-1, keepdims=True)
    yc = y - mu
    var = jnp.mean(yc * yc, axis=-1, keepdims=True)
    yn = yc * lax.rsqrt(var + NORM_EPS) * lnw_ref[...] + lnb_ref[...]
    o_ref[0] = yn * _sigmoid(yn)


def _conv(z3, conv_w, conv_b, ln_w, ln_b, tl, rb):
    bsz, seq, _ = z3.shape
    ch = conv_w.shape[1]
    hb = tl // CONV_HALO
    n_halo = seq // CONV_HALO
    vec = pl.BlockSpec((1, ch), lambda b, t: (0, 0))
    return pl.pallas_call(
        functools.partial(_conv_kernel, rb=rb),
        grid=(bsz, seq // tl),
        in_specs=[pl.BlockSpec((1, tl, ch), lambda b, t: (b, t, 0)),
                  pl.BlockSpec((1, CONV_HALO, ch), lambda b, t: (b, jnp.maximum(t * hb - 1, 0), 0)),
                  pl.BlockSpec((1, CONV_HALO, ch), lambda b, t: (b, jnp.minimum((t + 1) * hb, n_halo - 1), 0)),
                  pl.BlockSpec((CONV_K, ch), lambda b, t: (0, 0)),
                  vec, vec, vec],
        out_specs=pl.BlockSpec((1, tl, ch), lambda b, t: (b, t, 0)),
        out_shape=jax.ShapeDtypeStruct((bsz, seq, ch), F32),
        scratch_shapes=[pltpu.VMEM((tl + 2 * CONV_HALO, ch), BF16),
                        pltpu.VMEM((SUBLANES, rb + 2 * CONV_HALO - SUBLANES, ch), F32),
                        pltpu.VMEM((tl, ch), F32)],
        compiler_params=_params(2),
        name="conv",
    )(z3, z3, z3, conv_w, conv_b, ln_w, ln_b)


def _gla_direction(q_ref, v_ref, g_ref, o_ref, st, scr, backward):
    kd_s, qa_s, ka_s, qe_s, dec_s, kv_s, sb_s = scr
    rows, width = q_ref.shape[1], q_ref.shape[2]
    n_chunks, n_heads = rows // REC_CHUNK, width // REC_DK
    r_i = lax.broadcasted_iota(jnp.int32, (rows, rows), 0)
    c_i = lax.broadcasted_iota(jnp.int32, (rows, rows), 1)
    same_chunk = (r_i // REC_CHUNK) == (c_i // REC_CHUNK)
    tri = (same_chunk & ((c_i >= r_i) if backward else (c_i <= r_i))).astype(BF16)
    m_r = lax.broadcasted_iota(jnp.int32, (REC_CHUNK, REC_CHUNK), 0)
    m_c = lax.broadcasted_iota(jnp.int32, (REC_CHUNK, REC_CHUNK), 1)
    mask = (m_c >= m_r) if backward else (m_c <= m_r)
    ref_f = REC_CHUNK // 2 - 1
    last, ref = (0, REC_CHUNK - 1 - ref_f) if backward else (REC_CHUNK - 1, ref_f)

    b = _tri_sum(tri, g_ref[0])
    for c in range(n_chunks):
        rs = slice(c * REC_CHUNK, (c + 1) * REC_CHUNK)
        bc, qc = b[rs], q_ref[0, rs, :].astype(F32)
        b_last = bc[last:last + 1, :]
        b_ref = bc[ref:ref + 1, :]
        k = 1.0 - jnp.exp(g_ref[0, rs, :].astype(F32))
        kd_s[rs, :] = (k * jnp.exp(b_last - bc)).astype(BF16)
        qa_s[rs, :] = (qc * jnp.exp(bc - b_ref)).astype(BF16)
        ka_s[rs, :] = (k * jnp.exp(b_ref - bc)).astype(BF16)
        qe_s[rs, :] = (qc * jnp.exp(bc)).astype(BF16)
        dec_s[c:c + 1, :] = jnp.exp(b_last)

    tiles = [(c, h, slice(c * REC_CHUNK, (c + 1) * REC_CHUNK), slice(h * REC_DK, (h + 1) * REC_DK))
             for c in range(n_chunks) for h in range(n_heads)]
    scores = {}
    for c, h, rs, ls in tiles:
        s = _dot_nt(qa_s[rs, ls], ka_s[rs, ls])
        scores[c, h] = jnp.where(mask, s, 0.0).astype(BF16)
        kv_s[c, h] = _dot_tn(v_ref[0, rs, ls], kd_s[rs, ls])

    order = range(n_chunks - 1, -1, -1) if backward else range(n_chunks)
    for h in range(n_heads):
        ls = slice(h * REC_DK, (h + 1) * REC_DK)
        s = st[h]
        for c in order:
            sb_s[c, h] = s.astype(BF16)
            s = dec_s[c:c + 1, ls] * s + kv_s[c, h]
        st[h] = s

    for c, h, rs, ls in tiles:
        o_ref[0, rs, ls] = _dot(scores[c, h], v_ref[0, rs, ls]) + _dot_nt(qe_s[rs, ls], sb_s[c, h])


def _gla_kernel(qf_ref, vf_ref, gf_ref, qb_ref, vb_ref, gb_ref, s0f_ref, s0b_ref,
                of_ref, ob_ref, stf, stb, *scr):
    @pl.when(pl.program_id(1) == 0)
    def _():
        stf[...] = s0f_ref[0]
        stb[...] = s0b_ref[0]

    half = len(scr) // 2
    _gla_direction(qf_ref, vf_ref, gf_ref, of_ref, stf, scr[:half], backward=False)
    _gla_direction(qb_ref, vb_ref, gb_ref, ob_ref, stb, scr[half:], backward=True)


def _gla(z3, s0f, s0b, rows, width):
    bsz, seq, _ = z3.shape
    nb = seq // rows
    nh = width // REC_DK
    nc = rows // REC_CHUNK
    state = pltpu.VMEM((nh, REC_DK, REC_DK), F32)
    per_direction = ([pltpu.VMEM((rows, width), BF16)] * 4
                     + [pltpu.VMEM((max(nc, SUBLANES), width), F32),
                        pltpu.VMEM((nc, nh, REC_DK, REC_DK), F32),
                        pltpu.VMEM((nc, nh, REC_DK, REC_DK), BF16)])
    fwd = lambda col: pl.BlockSpec((1, rows, width), lambda b, n, col=col: (b, n, col))
    bwd = lambda col: pl.BlockSpec((1, rows, width), lambda b, n, col=col: (b, nb - 1 - n, col))
    st_spec = pl.BlockSpec((1, nh, REC_DK, REC_DK), lambda b, n: (b, 0, 0, 0))
    o_shape = jax.ShapeDtypeStruct((bsz, seq, width), F32)
    return pl.pallas_call(
        _gla_kernel,
        grid=(bsz, nb),
        in_specs=[fwd(1), fwd(3), fwd(4), bwd(1), bwd(3), bwd(5), st_spec, st_spec],
        out_specs=[pl.BlockSpec((1, rows, width), lambda b, n: (b, n, 0)),
                   pl.BlockSpec((1, rows, width), lambda b, n: (b, nb - 1 - n, 0))],
        out_shape=[o_shape, o_shape],
        scratch_shapes=[state, state] + per_direction * 2,
        compiler_params=_params(2),
        name="gla",
    )(z3, z3, z3, z3, z3, z3, s0f, s0b)


def _outproj_kernel(u_ref, of_ref, ob_ref, gs_ref, x_ref, mod_ref, rnw_ref, postw_ref, prew_ref, w_ref,
                    x1_ref, h2_ref):
    gw = u_ref.shape[1]
    o = of_ref[...] + ob_ref[...]
    heads = []
    for h in range(gw // REC_DK):
        oh = o[:, h * REC_DK:(h + 1) * REC_DK]
        heads.append(oh * lax.rsqrt(jnp.mean(oh * oh, axis=-1, keepdims=True) + NORM_EPS))
    on = jnp.concatenate(heads, axis=-1) * rnw_ref[...] * gs_ref[...]
    y = _dot(u_ref[...].astype(BF16), w_ref[0:gw, :]) + _dot(on.astype(BF16), w_ref[gw:2 * gw, :])
    x1 = x_ref[...] + mod_ref[0, 2:3, :] * _rms_norm(y, postw_ref[...])
    x1_ref[...] = x1
    h2 = _rms_norm(x1, prew_ref[...])
    h2_ref[...] = (h2 * (1.0 + mod_ref[0, 4:5, :]) + mod_ref[0, 3:4, :]).astype(BF16)


def _outproj(uconv, o_f, o_b, z2, x2, mod3, rnw, postw, prew, w_out_bf, tm, rows_per_batch):
    m, d = x2.shape
    gw = d // 2
    tiles_per_batch = rows_per_batch // tm
    half = pl.BlockSpec((tm, gw), lambda i: (i, 0))
    full = pl.BlockSpec((tm, d), lambda i: (i, 0))
    vec = lambda n: pl.BlockSpec((1, n), lambda i: (0, 0))
    return pl.pallas_call(
        _outproj_kernel,
        grid=(m // tm,),
        in_specs=[half, half, half,
                  pl.BlockSpec((tm, gw), lambda i: (i, 2)),
                  full,
                  pl.BlockSpec((1,) + mod3.shape[1:], lambda i: (i // tiles_per_batch, 0, 0)),
                  vec(gw), vec(d), vec(d),
                  pl.BlockSpec((d, d), lambda i: (0, 0))],
        out_specs=[full, full],
        out_shape=[jax.ShapeDtypeStruct((m, d), F32), jax.ShapeDtypeStruct((m, d), BF16)],
        compiler_params=_params(1),
        name="outproj",
    )(uconv, o_f, o_b, z2, x2, mod3, rnw, postw, prew, w_out_bf)


def _mlp_kernel(h_ref, x1_ref, mod_ref, postw_ref, wu_ref, wd_ref, o_ref, acc):
    k = pl.program_id(1)

    @pl.when(k == 0)
    def _():
        acc[...] = jnp.zeros_like(acc)

    hid = jnp.maximum(_dot(h_ref[...], wu_ref[...]), 0.0)
    acc[...] += _dot((hid * hid).astype(BF16), wd_ref[...])

    @pl.when(k == pl.num_programs(1) - 1)
    def _():
        o_ref[...] = x1_ref[...] + mod_ref[0, 5:6, :] * _rms_norm(acc[...], postw_ref[...])


def _mlp(h2, x1, mod3, postw, w_up_bf, w_down_bf, tm, tf, rows_per_batch):
    m, d = x1.shape
    ff = w_up_bf.shape[1]
    tiles_per_batch = rows_per_batch // tm
    return pl.pallas_call(
        _mlp_kernel,
        grid=(m // tm, ff // tf),
        in_specs=[pl.BlockSpec((tm, d), lambda i, k: (i, 0)),
                  pl.BlockSpec((tm, d), lambda i, k: (i, 0)),
                  pl.BlockSpec((1,) + mod3.shape[1:], lambda i, k: (i // tiles_per_batch, 0, 0)),
                  pl.BlockSpec((1, d), lambda i, k: (0, 0)),
                  pl.BlockSpec((d, tf), lambda i, k: (0, k)),
                  pl.BlockSpec((tf, d), lambda i, k: (k, 0))],
        out_specs=pl.BlockSpec((tm, d), lambda i, k: (i, 0)),
        out_shape=jax.ShapeDtypeStruct((m, d), F32),
        scratch_shapes=[pltpu.VMEM((tm, d), F32)],
        compiler_params=_params(2),
        name="mlp",
    )(h2, x1, mod3, postw, w_up_bf, w_down_bf)


def _tile(n, target):
    t = min(n, target)
    assert n % t == 0, (n, t)
    return t


def kernel(x, c, ctx, c_ctx, w_ada, b_ada, mix_pre_w, mix_post_w, mlp_pre_w, mlp_post_w, w_in, conv_w, conv_b,
           conv_ln_w, conv_ln_b, rec_lb_logits, rec_norm_w, w_out, w_up, w_down):
    bsz, seq, d = x.shape
    assert w_in.shape[0] == 1, "single-layer block"
    gw = d // 2
    assert gw % REC_DK == 0 and seq % REC_CHUNK == 0 and ctx.shape[1] % SUBLANES == 0
    m = bsz * seq

    cc = jnp.concatenate([c, c_ctx[None, :], jnp.zeros((SUBLANES - bsz - 1, d), c.dtype)], axis=0)
    mod = _ada(cc, w_ada[0], b_ada, _tile(w_ada.shape[2], 1024))
    mod3 = mod.reshape(SUBLANES, 6, d)

    w_in_bf = w_in[0].astype(BF16)
    w_out_bf = w_out[0].astype(BF16)
    w_up_bf = w_up[0].astype(BF16)
    w_down_bf = w_down[0].astype(BF16)
    lbl = rec_lb_logits.astype(F32)

    s0f, s0b = _ctx_states(ctx, mod3, mix_pre_w, lbl, w_in_bf, col0=2 * gw + 2 * gw)

    x2 = x.reshape(m, d)
    tm = _tile(seq, 512)
    z2 = _inproj(x2, mod3, mix_pre_w, lbl, w_in_bf, _tile(seq, 1024), _tile(gw, 256), seq)
    z3 = z2.reshape(bsz, seq, 6 * gw)

    tl = _tile(seq, 512)
    uconv = _conv(z3, conv_w[0], conv_b, conv_ln_w, conv_ln_b, tl, _tile(tl, 128))
    o_f, o_b = _gla(z3, s0f, s0b, _tile(seq, 256), gw)

    x1, h2 = _outproj(uconv.reshape(m, gw), o_f.reshape(m, gw), o_b.reshape(m, gw), z2, x2, mod3,
                      rec_norm_w, mix_post_w, mlp_pre_w, w_out_bf, _tile(seq, 256), seq)
    out = _mlp(h2, x1, mod3, mlp_post_w, w_up_bf, w_down_bf, tm, _tile(w_up.shape[2], 1024), seq)
    return out.reshape(bsz, seq, d)
```

```python
import functools

import jax
import jax.numpy as jnp
from jax import lax
from jax.experimental import pallas as pl
from jax.experimental.pallas import tpu as pltpu

NORM_EPS = 1e-6
REC_DK = 128
REC_CHUNK = 64
CONV_K = 31
CONV_HALO = 16
SUBLANES = 8
SUB_M = 256
VMEM_LIMIT_BYTES = 56 * 1024 * 1024

F32 = jnp.float32
BF16 = jnp.bfloat16


def _params(n_axes):
    return pltpu.CompilerParams(dimension_semantics=("arbitrary",) * n_axes,
                                vmem_limit_bytes=VMEM_LIMIT_BYTES)


def _sigmoid(x):
    return 1.0 / (1.0 + jnp.exp(-x))


def _rms_norm(xf, w):
    ms = jnp.mean(xf * xf, axis=-1, keepdims=True)
    return xf * lax.rsqrt(ms + NORM_EPS) * w


def _dot(a, b):
    return jnp.dot(a, b, preferred_element_type=F32)


def _dot_nt(a, b):
    return lax.dot_general(a, b, (((1,), (1,)), ((), ())), preferred_element_type=F32)


def _dot_tn(a, b):
    return lax.dot_general(a, b, (((0,), (0,)), ((), ())), preferred_element_type=F32)


def _tri_sum(tri, g):
    if g.dtype == BF16:
        return _dot(tri, g)
    g_hi = g.astype(BF16)
    g_lo = (g - g_hi.astype(F32)).astype(BF16)
    return _dot(tri, g_hi) + _dot(tri, g_lo)


def _lower_bound(lbl, direction):
    m = jnp.max(lbl, axis=0)
    e = jnp.exp(lbl - m[None])
    sm0 = e[0] / jnp.sum(e, axis=0)
    return sm0[direction:direction + 1]


def _log_forget(z, lb):
    return jnp.log2(lb + (1.0 - lb) * _sigmoid(z))


def _ada_kernel(c_ref, w_ref, b_ref, o_ref):
    c = c_ref[...]
    s = (c * _sigmoid(c)).astype(BF16)
    o_ref[...] = _dot(s, w_ref[...].astype(BF16)) + b_ref[...]


def _ada(cc, w_ada, b_ada, tn):
    rows, d = cc.shape
    n = w_ada.shape[1]
    return pl.pallas_call(
        _ada_kernel,
        grid=(n // tn,),
        in_specs=[pl.BlockSpec((rows, d), lambda j: (0, 0)),
                  pl.BlockSpec((d, tn), lambda j: (0, j)),
                  pl.BlockSpec((1, tn), lambda j: (0, j))],
        out_specs=pl.BlockSpec((rows, tn), lambda j: (0, j)),
        out_shape=jax.ShapeDtypeStruct((rows, n), F32),
        compiler_params=_params(1),
        name="ada",
    )(cc, w_ada, b_ada)


def _ctx_kernel(ctx_ref, mod_ref, npw_ref, lbl_ref, wi_ref, wf_ref, wb_ref, sf_ref, sb_ref, h_scr):
    @pl.when(pl.program_id(1) == 0)
    def _():
        h = _rms_norm(ctx_ref[0], npw_ref[...])
        h_scr[...] = (h * (1.0 + mod_ref[0, 1:2, :]) + mod_ref[0, 0:1, :]).astype(BF16)

    hb = h_scr[...]
    lc = hb.shape[0]
    v = _dot(hb, wi_ref[...]).astype(BF16)
    lbl = lbl_ref[...]
    g_f = _log_forget(_dot(hb, wf_ref[...]), _lower_bound(lbl, 0))
    g_b = _log_forget(_dot(hb, wb_ref[...]), _lower_bound(lbl, 1))
    row = lax.broadcasted_iota(jnp.int32, (lc, lc), 0)
    col = lax.broadcasted_iota(jnp.int32, (lc, lc), 1)
    incl = (col <= row).astype(BF16)
    excl = (col < row).astype(BF16)
    b_f = _tri_sum(incl, g_f)
    kd_f = (1.0 - jnp.exp2(g_f)) * jnp.exp2(b_f[lc - 1:lc, :] - b_f)
    c_b = _tri_sum(excl, g_b)
    kd_b = (1.0 - jnp.exp2(g_b)) * jnp.exp2(c_b)
    sf_ref[0, 0] = _dot_tn(v, kd_f.astype(BF16))
    sb_ref[0, 0] = _dot_tn(v, kd_b.astype(BF16))


def _ctx_states(ctx, mod3, npw, lbl, w_in_bf, col0):
    bsz, lc, d = ctx.shape
    nh = lbl.shape[-1] // REC_DK
    cb = col0 // REC_DK
    wspec = lambda off: pl.BlockSpec((d, REC_DK), lambda b, h, off=off: (0, cb + off * nh + h))
    st_spec = pl.BlockSpec((1, 1, REC_DK, REC_DK), lambda b, h: (b, h, 0, 0))
    st_shape = jax.ShapeDtypeStruct((bsz, nh, REC_DK, REC_DK), F32)
    return pl.pallas_call(
        _ctx_kernel,
        grid=(bsz, nh),
        in_specs=[pl.BlockSpec((1, lc, d), lambda b, h: (b, 0, 0)),
                  pl.BlockSpec((1,) + mod3.shape[1:], lambda b, h: (bsz, 0, 0)),
                  pl.BlockSpec((1, d), lambda b, h: (0, 0)),
                  pl.BlockSpec(lbl.shape[:2] + (REC_DK,), lambda b, h: (0, 0, h)),
                  wspec(0), wspec(1), wspec(2)],
        out_specs=[st_spec, st_spec],
        out_shape=[st_shape, st_shape],
        scratch_shapes=[pltpu.VMEM((lc, d), BF16)],
        compiler_params=_params(2),
        name="ctx_states",
    )(ctx, mod3, npw, lbl, w_in_bf, w_in_bf, w_in_bf)


def _project_tiles(h_scr, w_ref, tn, epilogue):
    for m0 in range(0, h_scr.shape[0], SUB_M):
        rs = slice(m0, m0 + SUB_M)
        for n0 in range(0, w_ref.shape[1], tn):
            cs = slice(n0, n0 + tn)
            epilogue(rs, cs, _dot(h_scr[rs, :], w_ref[:, cs]))


def _inproj_kernel(x_ref, mod_ref, npw_ref, lbl_ref, w_ref, o_ref, h_scr, a_scr, *, tn):
    j = pl.program_id(1)

    @pl.when(j == 0)
    def _():
        h = _rms_norm(x_ref[...], npw_ref[...])
        h_scr[...] = (h * (1.0 + mod_ref[0, 1:2, :]) + mod_ref[0, 0:1, :]).astype(BF16)

        def park(rs, cs, z):
            a_scr[rs, cs] = z
        _project_tiles(h_scr, w_ref, tn, park)

    @pl.when(j == 1)
    def _():
        def glu(rs, cs, z):
            o_ref[rs, cs] = (a_scr[rs, cs] * _sigmoid(z)).astype(o_ref.dtype)
        _project_tiles(h_scr, w_ref, tn, glu)

    @pl.when((j == 2) | (j == 3))
    def _():
        def swish(rs, cs, z):
            o_ref[rs, cs] = (z * _sigmoid(z)).astype(o_ref.dtype)
        _project_tiles(h_scr, w_ref, tn, swish)

    @pl.when(j == 4)
    def _():
        def values(rs, cs, z):
            o_ref[rs, cs] = z.astype(o_ref.dtype)
        _project_tiles(h_scr, w_ref, tn, values)

    @pl.when(j >= 5)
    def _():
        lbl = lbl_ref[...]
        lb = jnp.where(j == 5, _lower_bound(lbl, 0), _lower_bound(lbl, 1))

        def log_gate(rs, cs, z):
            o_ref[rs, cs] = _log_forget(z, lb[:, cs]).astype(o_ref.dtype)
        _project_tiles(h_scr, w_ref, tn, log_gate)


def _inproj(x2, mod3, npw, lbl, w_in_bf, tm, tn, rows_per_batch):
    m, d = x2.shape
    gw = d // 2
    n_groups = w_in_bf.shape[1] // gw
    tiles_per_batch = rows_per_batch // tm
    return pl.pallas_call(
        functools.partial(_inproj_kernel, tn=tn),
        grid=(m // tm, n_groups),
        in_specs=[pl.BlockSpec((tm, d), lambda i, j: (i, 0)),
                  pl.BlockSpec((1,) + mod3.shape[1:], lambda i, j: (i // tiles_per_batch, 0, 0)),
                  pl.BlockSpec((1, d), lambda i, j: (0, 0)),
                  pl.BlockSpec(lbl.shape, lambda i, j: (0, 0, 0)),
                  pl.BlockSpec((d, gw), lambda i, j: (0, j))],
        out_specs=pl.BlockSpec((tm, gw), lambda i, j: (i, jnp.maximum(j - 1, 0))),
        out_shape=jax.ShapeDtypeStruct((m, (n_groups - 1) * gw), BF16),
        scratch_shapes=[pltpu.VMEM((tm, d), BF16), pltpu.VMEM((tm, gw), F32)],
        compiler_params=_params(2),
        name="inproj",
    )(x2, mod3, npw, lbl, w_in_bf)


def _conv_kernel(u_ref, ul_ref, ur_ref, cw_ref, cb_ref, lnw_ref, lnb_ref, o_ref, buf, sh_scr, y_scr, *, rb):
    t = pl.program_id(1)
    tl, ch = u_ref.shape[1], u_ref.shape[2]
    buf[pl.ds(CONV_HALO, tl), :] = u_ref[0]
    buf[pl.ds(0, CONV_HALO), :] = jnp.where(t > 0, ul_ref[0], jnp.zeros_like(ul_ref[0]))
    buf[pl.ds(CONV_HALO + tl, CONV_HALO), :] = jnp.where(t < pl.num_programs(1) - 1, ur_ref[0],
                                                          jnp.zeros_like(ur_ref[0]))

    shift0 = CONV_HALO - CONV_K // 2
    win = rb + 2 * CONV_HALO
    span = win - SUBLANES
    r_i = lax.broadcasted_iota(jnp.int32, (span, win), 0)
    c_i = lax.broadcasted_iota(jnp.int32, (span, win), 1)
    shifts = [(c_i == r_i + p).astype(BF16) for p in range(SUBLANES)]

    def row_block(i, carry):
        r0 = pl.multiple_of(i * rb, rb)
        w = buf[pl.ds(r0, win), :]
        for p in range(SUBLANES):
            sh_scr[p] = _dot(shifts[p], w)
        for c0 in range(0, ch, 128):
            acc = jnp.zeros((rb, 128), F32)
            for k in range(CONV_K):
                p, a0 = (k + shift0) % SUBLANES, (k + shift0) // SUBLANES * SUBLANES
                acc = acc + sh_scr[p, a0:a0 + rb, c0:c0 + 128] * cw_ref[k:k + 1, c0:c0 + 128]
            y_scr[pl.ds(r0, rb), c0:c0 + 128] = acc
        return carry

    lax.fori_loop(0, tl // rb, row_block, 0)

    y = y_scr[...] + cb_ref[...]
    mu = jnp.mean(y, axis=-1, keepdims=True)
    yc = y - mu
    var = jnp.mean(yc * yc, axis=-1, keepdims=True)
    yn = yc * lax.rsqrt(var + NORM_EPS) * lnw_ref[...] + lnb_ref[...]
    o_ref[0] = (yn * _sigmoid(yn)).astype(o_ref.dtype)


def _conv(z3, conv_w, conv_b, ln_w, ln_b, tl, rb):
    bsz, seq, _ = z3.shape
    ch = conv_w.shape[1]
    hb = tl // CONV_HALO
    n_halo = seq // CONV_HALO
    vec = pl.BlockSpec((1, ch), lambda b, t: (0, 0))
    return pl.pallas_call(
        functools.partial(_conv_kernel, rb=rb),
        grid=(bsz, seq // tl),
        in_specs=[pl.BlockSpec((1, tl, ch), lambda b, t: (b, t, 0)),
                  pl.BlockSpec((1, CONV_HALO, ch), lambda b, t: (b, jnp.maximum(t * hb - 1, 0), 0)),
                  pl.BlockSpec((1, CONV_HALO, ch), lambda b, t: (b, jnp.minimum((t + 1) * hb, n_halo - 1), 0)),
                  pl.BlockSpec((CONV_K, ch), lambda b, t: (0, 0)),
                  vec, vec, vec],
        out_specs=pl.BlockSpec((1, tl, ch), lambda b, t: (b, t, 0)),
        out_shape=jax.ShapeDtypeStruct((bsz, seq, ch), BF16),
        scratch_shapes=[pltpu.VMEM((tl + 2 * CONV_HALO, ch), BF16),
                        pltpu.VMEM((SUBLANES, rb + 2 * CONV_HALO - SUBLANES, ch), F32),
                        pltpu.VMEM((tl, ch), F32)],
        compiler_params=_params(2),
        name="conv",
    )(z3, z3, z3, conv_w, conv_b, ln_w, ln_b)


class _Scan:
    def __init__(self, q_ref, v_ref, g_ref, o_ref, st, scr, backward):
        self.q_ref, self.v_ref, self.g_ref, self.o_ref, self.st = q_ref, v_ref, g_ref, o_ref, st
        self.kd_s, self.qa_s, self.ka_s, self.qe_s, self.dec_s, self.kv_s, self.sb_s = scr
        rows, width = q_ref.shape[1], q_ref.shape[2]
        self.n_chunks, self.n_heads = rows // REC_CHUNK, width // REC_DK
        r_i = lax.broadcasted_iota(jnp.int32, (rows, rows), 0)
        c_i = lax.broadcasted_iota(jnp.int32, (rows, rows), 1)
        same_chunk = (r_i // REC_CHUNK) == (c_i // REC_CHUNK)
        tri = (same_chunk & ((c_i >= r_i) if backward else (c_i <= r_i))).astype(BF16)
        m_r = lax.broadcasted_iota(jnp.int32, (REC_CHUNK, REC_CHUNK), 0)
        m_c = lax.broadcasted_iota(jnp.int32, (REC_CHUNK, REC_CHUNK), 1)
        self.mask = (m_c >= m_r) if backward else (m_c <= m_r)
        ref_f = REC_CHUNK // 2 - 1
        self.last, self.ref = (0, REC_CHUNK - 1 - ref_f) if backward else (REC_CHUNK - 1, ref_f)
        self.order = list(range(self.n_chunks - 1, -1, -1) if backward else range(self.n_chunks))
        self.b = _tri_sum(tri, g_ref[0])
        self.scores = {}

    def _rows(self, c):
        return slice(c * REC_CHUNK, (c + 1) * REC_CHUNK)

    def _heads(self):
        return [(h, slice(h * REC_DK, (h + 1) * REC_DK)) for h in range(self.n_heads)]

    def prep(self, c):
        rs = self._rows(c)
        bc, qc = self.b[rs], self.q_ref[0, rs, :].astype(F32)
        b_last = bc[self.last:self.last + 1, :]
        b_ref = bc[self.ref:self.ref + 1, :]
        k = 1.0 - jnp.exp2(self.g_ref[0, rs, :].astype(F32))
        self.kd_s[rs, :] = (k * jnp.exp2(b_last - bc)).astype(BF16)
        self.qa_s[rs, :] = (qc * jnp.exp2(bc - b_ref)).astype(BF16)
        self.ka_s[rs, :] = (k * jnp.exp2(b_ref - bc)).astype(BF16)
        self.qe_s[rs, :] = (qc * jnp.exp2(bc)).astype(BF16)
        self.dec_s[c:c + 1, :] = jnp.exp2(b_last)

    def intra(self, c):
        rs = self._rows(c)
        for h, ls in self._heads():
            s = _dot_nt(self.qa_s[rs, ls], self.ka_s[rs, ls])
            self.scores[c, h] = jnp.where(self.mask, s, 0.0).astype(BF16)
            self.kv_s[c, h] = _dot_tn(self.v_ref[0, rs, ls], self.kd_s[rs, ls])

    def recur(self):
        for h, ls in self._heads():
            s = self.st[h]
            for c in self.order:
                self.sb_s[c, h] = s.astype(BF16)
                s = self.dec_s[c:c + 1, ls] * s + self.kv_s[c, h]
            self.st[h] = s

    def out(self, c):
        rs = self._rows(c)
        for h, ls in self._heads():
            self.o_ref[0, rs, ls] = (_dot(self.scores[c, h], self.v_ref[0, rs, ls])
                                     + _dot_nt(self.qe_s[rs, ls], self.sb_s[c, h]))


def _gla_kernel(qf_ref, vf_ref, gf_ref, qb_ref, vb_ref, gb_ref, s0f_ref, s0b_ref,
                of_ref, ob_ref, stf, stb, *scr):
    @pl.when(pl.program_id(1) == 0)
    def _():
        stf[...] = s0f_ref[0]
        stb[...] = s0b_ref[0]

    half = len(scr) // 2
    fwd = _Scan(qf_ref, vf_ref, gf_ref, of_ref, stf, scr[:half], backward=False)
    bwd = _Scan(qb_ref, vb_ref, gb_ref, ob_ref, stb, scr[half:], backward=True)
    n = fwd.n_chunks
    for i in range(n):
        fwd.prep(fwd.order[i])
        if i > 0:
            bwd.intra(bwd.order[i - 1])
        bwd.prep(bwd.order[i])
        fwd.intra(fwd.order[i])
    bwd.intra(bwd.order[n - 1])
    fwd.recur()
    bwd.recur()
    for i in range(n):
        fwd.out(fwd.order[i])
        bwd.out(bwd.order[i])


def _gla(z3, s0f, s0b, rows, width):
    bsz, seq, _ = z3.shape
    nb = seq // rows
    nh = width // REC_DK
    nc = rows // REC_CHUNK
    state = pltpu.VMEM((nh, REC_DK, REC_DK), F32)
    per_direction = ([pltpu.VMEM((rows, width), BF16)] * 4
                     + [pltpu.VMEM((max(nc, SUBLANES), width), F32),
                        pltpu.VMEM((nc, nh, REC_DK, REC_DK), F32),
                        pltpu.VMEM((nc, nh, REC_DK, REC_DK), BF16)])
    fwd = lambda col: pl.BlockSpec((1, rows, width), lambda b, n, col=col: (b, n, col))
    bwd = lambda col: pl.BlockSpec((1, rows, width), lambda b, n, col=col: (b, nb - 1 - n, col))
    st_spec = pl.BlockSpec((1, nh, REC_DK, REC_DK), lambda b, n: (b, 0, 0, 0))
    o_shape = jax.ShapeDtypeStruct((bsz, seq, width), F32)
    return pl.pallas_call(
        _gla_kernel,
        grid=(bsz, nb),
        in_specs=[fwd(1), fwd(3), fwd(4), bwd(1), bwd(3), bwd(5), st_spec, st_spec],
        out_specs=[pl.BlockSpec((1, rows, width), lambda b, n: (b, n, 0)),
                   pl.BlockSpec((1, rows, width), lambda b, n: (b, nb - 1 - n, 0))],
        out_shape=[o_shape, o_shape],
        scratch_shapes=[state, state] + per_direction * 2,
        compiler_params=_params(2),
        name="gla",
    )(z3, z3, z3, z3, z3, z3, s0f, s0b)


def _outproj_kernel(u_ref, of_ref, ob_ref, gs_ref, x_ref, mod_ref, rnw_ref, postw_ref, prew_ref, w_ref,
                    x1_ref, h2_ref):
    gw = u_ref.shape[1]
    for m0 in range(0, u_ref.shape[0], SUB_M):
        rs = slice(m0, m0 + SUB_M)
        o = of_ref[rs, :] + ob_ref[rs, :]
        heads = []
        for h in range(gw // REC_DK):
            oh = o[:, h * REC_DK:(h + 1) * REC_DK]
            heads.append(oh * lax.rsqrt(jnp.mean(oh * oh, axis=-1, keepdims=True) + NORM_EPS))
        on = jnp.concatenate(heads, axis=-1) * rnw_ref[...] * gs_ref[rs, :]
        y = _dot(u_ref[rs, :].astype(BF16), w_ref[0:gw, :]) + _dot(on.astype(BF16), w_ref[gw:2 * gw, :])
        x1 = x_ref[rs, :] + mod_ref[0, 2:3, :] * _rms_norm(y, postw_ref[...])
        x1_ref[rs, :] = x1
        h2 = _rms_norm(x1, prew_ref[...])
        h2_ref[rs, :] = (h2 * (1.0 + mod_ref[0, 4:5, :]) + mod_ref[0, 3:4, :]).astype(BF16)


def _outproj(uconv, o_f, o_b, z2, x2, mod3, rnw, postw, prew, w_out_bf, tm, rows_per_batch):
    m, d = x2.shape
    gw = d // 2
    tiles_per_batch = rows_per_batch // tm
    half = pl.BlockSpec((tm, gw), lambda i: (i, 0))
    full = pl.BlockSpec((tm, d), lambda i: (i, 0))
    vec = lambda n: pl.BlockSpec((1, n), lambda i: (0, 0))
    return pl.pallas_call(
        _outproj_kernel,
        grid=(m // tm,),
        in_specs=[half, half, half,
                  pl.BlockSpec((tm, gw), lambda i: (i, 2)),
                  full,
                  pl.BlockSpec((1,) + mod3.shape[1:], lambda i: (i // tiles_per_batch, 0, 0)),
                  vec(gw), vec(d), vec(d),
                  pl.BlockSpec((d, d), lambda i: (0, 0))],
        out_specs=[full, full],
        out_shape=[jax.ShapeDtypeStruct((m, d), F32), jax.ShapeDtypeStruct((m, d), BF16)],
        compiler_params=_params(1),
        name="outproj",
    )(uconv, o_f, o_b, z2, x2, mod3, rnw, postw, prew, w_out_bf)


def _mlp_kernel(h_ref, x1_ref, mod_ref, postw_ref, wu_ref, wd_ref, o_ref, acc):
    k = pl.program_id(1)
    last = pl.num_programs(1) - 1

    def partial_sum(rs):
        hid = jnp.maximum(_dot(h_ref[rs, :], wu_ref[...]), 0.0)
        return _dot((hid * hid).astype(BF16), wd_ref[...])

    @pl.when(k == 0)
    def _():
        acc[...] = partial_sum(slice(None))

    @pl.when((k > 0) & (k < last))
    def _():
        acc[...] += partial_sum(slice(None))

    @pl.when(k == last)
    def _():
        for m0 in range(0, acc.shape[0], SUB_M):
            rs = slice(m0, m0 + SUB_M)
            y = acc[rs, :] + partial_sum(rs)
            o_ref[rs, :] = x1_ref[rs, :] + mod_ref[0, 5:6, :] * _rms_norm(y, postw_ref[...])


def _mlp(h2, x1, mod3, postw, w_up_bf, w_down_bf, tm, tf, rows_per_batch):
    m, d = x1.shape
    ff = w_up_bf.shape[1]
    assert ff // tf >= 2, "the first and last hidden chunks take different branches"
    tiles_per_batch = rows_per_batch // tm
    return pl.pallas_call(
        _mlp_kernel,
        grid=(m // tm, ff // tf),
        in_specs=[pl.BlockSpec((tm, d), lambda i, k: (i, 0)),
                  pl.BlockSpec((tm, d), lambda i, k: (i, 0)),
                  pl.BlockSpec((1,) + mod3.shape[1:], lambda i, k: (i // tiles_per_batch, 0, 0)),
                  pl.BlockSpec((1, d), lambda i, k: (0, 0)),
                  pl.BlockSpec((d, tf), lambda i, k: (0, k)),
                  pl.BlockSpec((tf, d), lambda i, k: (k, 0))],
        out_specs=pl.BlockSpec((tm, d), lambda i, k: (i, 0)),
        out_shape=jax.ShapeDtypeStruct((m, d), F32),
        scratch_shapes=[pltpu.VMEM((tm, d), F32)],
        compiler_params=_params(2),
        name="mlp",
    )(h2, x1, mod3, postw, w_up_bf, w_down_bf)


def _tile(n, target):
    t = min(n, target)
    assert n % t == 0, (n, t)
    return t


def kernel(x, c, ctx, c_ctx, w_ada, b_ada, mix_pre_w, mix_post_w, mlp_pre_w, mlp_post_w, w_in, conv_w, conv_b,
           conv_ln_w, conv_ln_b, rec_lb_logits, rec_norm_w, w_out, w_up, w_down):
    bsz, seq, d = x.shape
    assert w_in.shape[0] == 1, "single-layer block"
    gw = d // 2
    assert gw % REC_DK == 0 and seq % REC_CHUNK == 0 and ctx.shape[1] % SUBLANES == 0
    m = bsz * seq

    cc = jnp.concatenate([c, c_ctx[None, :], jnp.zeros((SUBLANES - bsz - 1, d), c.dtype)], axis=0)
    mod = _ada(cc, w_ada[0], b_ada, _tile(w_ada.shape[2], 1024))
    mod3 = mod.reshape(SUBLANES, 6, d)

    w_in_bf = w_in[0].astype(BF16)
    w_out_bf = w_out[0].astype(BF16)
    w_up_bf = w_up[0].astype(BF16)
    w_down_bf = w_down[0].astype(BF16)
    lbl = rec_lb_logits.astype(F32)

    s0f, s0b = _ctx_states(ctx, mod3, mix_pre_w, lbl, w_in_bf, col0=2 * gw + 2 * gw)

    x2 = x.reshape(m, d)
    tm = _tile(seq, 512)
    z2 = _inproj(x2, mod3, mix_pre_w, lbl, w_in_bf, _tile(seq, 1024), _tile(gw, 512), seq)
    z3 = z2.reshape(bsz, seq, 6 * gw)

    tl = _tile(seq, 512)
    uconv = _conv(z3, conv_w[0], conv_b, conv_ln_w, conv_ln_b, tl, _tile(tl, 128))
    o_f, o_b = _gla(z3, s0f, s0b, _tile(seq, 256), gw)

    x1, h2 = _outproj(uconv.reshape(m, gw), o_f.reshape(m, gw), o_b.reshape(m, gw), z2, x2, mod3,
                      rec_norm_w, mix_post_w, mlp_pre_w, w_out_bf, _tile(seq, 512), seq)
    out = _mlp(h2, x1, mod3, mlp_post_w, w_up_bf, w_down_bf, tm, _tile(w_up.shape[2], 1024), seq)
    return out.reshape(bsz, seq, d)
```

```python
import functools

import jax
import jax.numpy as jnp
from jax import lax
from jax.experimental import pallas as pl
from jax.experimental.pallas import tpu as pltpu

NORM_EPS = 1e-6
REC_DK = 128
REC_CHUNK = 64
CONV_K = 31
CONV_HALO = 16
CONV_SLAB = 256
SUBLANES = 8
SUB_M = 256
VMEM_LIMIT_BYTES = 56 * 1024 * 1024

F32 = jnp.float32
BF16 = jnp.bfloat16


def _params(n_axes):
    return pltpu.CompilerParams(dimension_semantics=("arbitrary",) * n_axes,
                                vmem_limit_bytes=VMEM_LIMIT_BYTES)


def _sigmoid(x):
    return 1.0 / (1.0 + jnp.exp(-x))


def _rms_norm(xf, w):
    ms = jnp.mean(xf * xf, axis=-1, keepdims=True)
    return xf * lax.rsqrt(ms + NORM_EPS) * w


def _dot(a, b):
    return jnp.dot(a, b, preferred_element_type=F32)


def _dot_nt(a, b):
    return lax.dot_general(a, b, (((1,), (1,)), ((), ())), preferred_element_type=F32)


def _dot_tn(a, b):
    return lax.dot_general(a, b, (((0,), (0,)), ((), ())), preferred_element_type=F32)


def _tri_sum(tri, g):
    if g.dtype == BF16:
        return _dot(tri, g)
    g_hi = g.astype(BF16)
    g_lo = (g - g_hi.astype(F32)).astype(BF16)
    return _dot(tri, g_hi) + _dot(tri, g_lo)


def _lower_bound(lbl, direction):
    m = jnp.max(lbl, axis=0)
    e = jnp.exp(lbl - m[None])
    sm0 = e[0] / jnp.sum(e, axis=0)
    return sm0[direction:direction + 1]


def _log_forget(z, lb):
    return jnp.log2(lb + (1.0 - lb) * _sigmoid(z))


def _ada_kernel(c_ref, w_ref, b_ref, o_ref):
    c = c_ref[...]
    s = (c * _sigmoid(c)).astype(BF16)
    o_ref[...] = _dot(s, w_ref[...].astype(BF16)) + b_ref[...]


def _ada(cc, w_ada, b_ada, tn):
    rows, d = cc.shape
    n = w_ada.shape[1]
    return pl.pallas_call(
        _ada_kernel,
        grid=(n // tn,),
        in_specs=[pl.BlockSpec((rows, d), lambda j: (0, 0)),
                  pl.BlockSpec((d, tn), lambda j: (0, j)),
                  pl.BlockSpec((1, tn), lambda j: (0, j))],
        out_specs=pl.BlockSpec((rows, tn), lambda j: (0, j)),
        out_shape=jax.ShapeDtypeStruct((rows, n), F32),
        compiler_params=_params(1),
        name="ada",
    )(cc, w_ada, b_ada)


def _ctx_kernel(ctx_ref, mod_ref, npw_ref, lbl_ref, wi_ref, wf_ref, wb_ref, sf_ref, sb_ref, h_scr):
    @pl.when(pl.program_id(1) == 0)
    def _():
        h = _rms_norm(ctx_ref[0], npw_ref[...])
        h_scr[...] = (h * (1.0 + mod_ref[0, 1:2, :]) + mod_ref[0, 0:1, :]).astype(BF16)

    hb = h_scr[...]
    lc = hb.shape[0]
    v = _dot(hb, wi_ref[...].astype(BF16)).astype(BF16)
    lbl = lbl_ref[...]
    g_f = _log_forget(_dot(hb, wf_ref[...].astype(BF16)), _lower_bound(lbl, 0))
    g_b = _log_forget(_dot(hb, wb_ref[...].astype(BF16)), _lower_bound(lbl, 1))
    row = lax.broadcasted_iota(jnp.int32, (lc, lc), 0)
    col = lax.broadcasted_iota(jnp.int32, (lc, lc), 1)
    incl = (col <= row).astype(BF16)
    excl = (col < row).astype(BF16)
    b_f = _tri_sum(incl, g_f)
    kd_f = (1.0 - jnp.exp2(g_f)) * jnp.exp2(b_f[lc - 1:lc, :] - b_f)
    c_b = _tri_sum(excl, g_b)
    kd_b = (1.0 - jnp.exp2(g_b)) * jnp.exp2(c_b)
    sf_ref[0, 0] = _dot_tn(v, kd_f.astype(BF16))
    sb_ref[0, 0] = _dot_tn(v, kd_b.astype(BF16))


def _ctx_states(ctx, mod3, npw, lbl, w_in, col0):
    bsz, lc, d = ctx.shape
    nh = lbl.shape[-1] // REC_DK
    cb = col0 // REC_DK
    wspec = lambda off: pl.BlockSpec((d, REC_DK), lambda b, h, off=off: (0, cb + off * nh + h))
    st_spec = pl.BlockSpec((1, 1, REC_DK, REC_DK), lambda b, h: (b, h, 0, 0))
    st_shape = jax.ShapeDtypeStruct((bsz, nh, REC_DK, REC_DK), F32)
    return pl.pallas_call(
        _ctx_kernel,
        grid=(bsz, nh),
        in_specs=[pl.BlockSpec((1, lc, d), lambda b, h: (b, 0, 0)),
                  pl.BlockSpec((1,) + mod3.shape[1:], lambda b, h: (bsz, 0, 0)),
                  pl.BlockSpec((1, d), lambda b, h: (0, 0)),
                  pl.BlockSpec(lbl.shape[:2] + (REC_DK,), lambda b, h: (0, 0, h)),
                  wspec(0), wspec(1), wspec(2)],
        out_specs=[st_spec, st_spec],
        out_shape=[st_shape, st_shape],
        scratch_shapes=[pltpu.VMEM((lc, d), BF16)],
        compiler_params=_params(2),
        name="ctx_states",
    )(ctx, mod3, npw, lbl, w_in, w_in, w_in)


def _project_tiles(h_scr, w_ref, wb_scr, tn, epilogue):
    for m0 in range(0, h_scr.shape[0], SUB_M):
        rs = slice(m0, m0 + SUB_M)
        for n0 in range(0, w_ref.shape[1], tn):
            cs = slice(n0, n0 + tn)
            if m0 == 0:
                wb_scr[:, cs] = w_ref[:, cs].astype(BF16)
            epilogue(rs, cs, _dot(h_scr[rs, :], wb_scr[:, cs]))


def _inproj_kernel(x_ref, mod_ref, npw_ref, lbl_ref, w_ref, o_ref, h_scr, a_scr, wb_scr, *, tn):
    j = pl.program_id(1)

    @pl.when(j == 0)
    def _():
        h = _rms_norm(x_ref[...], npw_ref[...])
        h_scr[...] = (h * (1.0 + mod_ref[0, 1:2, :]) + mod_ref[0, 0:1, :]).astype(BF16)

        def park(rs, cs, z):
            a_scr[rs, cs] = z
        _project_tiles(h_scr, w_ref, wb_scr, tn, park)

    @pl.when(j == 1)
    def _():
        def glu(rs, cs, z):
            o_ref[rs, cs] = (a_scr[rs, cs] * _sigmoid(z)).astype(o_ref.dtype)
        _project_tiles(h_scr, w_ref, wb_scr, tn, glu)

    @pl.when((j == 2) | (j == 3))
    def _():
        def swish(rs, cs, z):
            o_ref[rs, cs] = (z * _sigmoid(z)).astype(o_ref.dtype)
        _project_tiles(h_scr, w_ref, wb_scr, tn, swish)

    @pl.when(j == 4)
    def _():
        def values(rs, cs, z):
            o_ref[rs, cs] = z.astype(o_ref.dtype)
        _project_tiles(h_scr, w_ref, wb_scr, tn, values)

    @pl.when(j >= 5)
    def _():
        lbl = lbl_ref[...]
        lb = jnp.where(j == 5, _lower_bound(lbl, 0), _lower_bound(lbl, 1))

        def log_gate(rs, cs, z):
            o_ref[rs, cs] = _log_forget(z, lb[:, cs]).astype(o_ref.dtype)
        _project_tiles(h_scr, w_ref, wb_scr, tn, log_gate)


def _inproj(x2, mod3, npw, lbl, w_in, tm, tn, rows_per_batch):
    m, d = x2.shape
    gw = d // 2
    n_groups = w_in.shape[1] // gw
    tiles_per_batch = rows_per_batch // tm
    return pl.pallas_call(
        functools.partial(_inproj_kernel, tn=tn),
        grid=(m // tm, n_groups),
        in_specs=[pl.BlockSpec((tm, d), lambda i, j: (i, 0)),
                  pl.BlockSpec((1,) + mod3.shape[1:], lambda i, j: (i // tiles_per_batch, 0, 0)),
                  pl.BlockSpec((1, d), lambda i, j: (0, 0)),
                  pl.BlockSpec(lbl.shape, lambda i, j: (0, 0, 0)),
                  pl.BlockSpec((d, gw), lambda i, j: (0, j))],
        out_specs=pl.BlockSpec((tm, gw), lambda i, j: (i, jnp.maximum(j - 1, 0))),
        out_shape=jax.ShapeDtypeStruct((m, (n_groups - 1) * gw), BF16),
        scratch_shapes=[pltpu.VMEM((tm, d), BF16), pltpu.VMEM((tm, gw), F32), pltpu.VMEM((d, gw), BF16)],
        compiler_params=_params(2),
        name="inproj",
    )(x2, mod3, npw, lbl, w_in)


def _conv_kernel(u_ref, ul_ref, ur_ref, cw_ref, cb_ref, lnw_ref, lnb_ref, *rest, rb, n_cast):
    cast_in, o_ref, cast_out = rest[:n_cast], rest[n_cast], rest[n_cast + 1:2 * n_cast + 1]
    buf, y_scr = rest[2 * n_cast + 1:]
    for src, dst in zip(cast_in, cast_out):
        dst[...] = src[...].astype(dst.dtype)

    t = pl.program_id(1)
    tl, ch = u_ref.shape[1], u_ref.shape[2]
    buf[pl.ds(CONV_HALO, tl), :] = u_ref[0]
    buf[pl.ds(0, CONV_HALO), :] = jnp.where(t > 0, ul_ref[0], jnp.zeros_like(ul_ref[0]))
    buf[pl.ds(CONV_HALO + tl, CONV_HALO), :] = jnp.where(t < pl.num_programs(1) - 1, ur_ref[0],
                                                          jnp.zeros_like(ur_ref[0]))

    shift0 = CONV_HALO - CONV_K // 2
    win = rb + 2 * CONV_HALO
    span = win - SUBLANES
    r_i = lax.broadcasted_iota(jnp.int32, (span, win), 0)
    c_i = lax.broadcasted_iota(jnp.int32, (span, win), 1)
    shifts = [(c_i == r_i + p).astype(BF16) for p in range(SUBLANES)]

    def row_block(i, carry):
        r0 = pl.multiple_of(i * rb, rb)

        for l0 in range(0, ch, CONV_SLAB):
            ls = slice(l0, l0 + CONV_SLAB)
            w = buf[pl.ds(r0, win), ls]
            acc = jnp.zeros((rb, CONV_SLAB), F32)
            for p in range(SUBLANES):
                sp = _dot(shifts[p], w)
                for k in range(CONV_K):
                    if (k + shift0) % SUBLANES == p:
                        a0 = (k + shift0) // SUBLANES * SUBLANES
                        acc = acc + sp[a0:a0 + rb, :] * cw_ref[k:k + 1, ls]
            y_scr[pl.ds(r0, rb), ls] = acc
        return carry

    lax.fori_loop(0, tl // rb, row_block, 0)

    y = y_scr[...] + cb_ref[...]
    mu = jnp.mean(y, axis=-1, keepdims=True)
    yc = y - mu
    var = jnp.mean(yc * yc, axis=-1, keepdims=True)
    yn = yc * lax.rsqrt(var + NORM_EPS) * lnw_ref[...] + lnb_ref[...]
    o_ref[0] = (yn * _sigmoid(yn)).astype(o_ref.dtype)


def _conv(z3, conv_w, conv_b, ln_w, ln_b, tl, rb, cast_weights):
    bsz, seq, _ = z3.shape
    ch = conv_w.shape[1]
    hb = tl // CONV_HALO
    n_halo = seq // CONV_HALO
    nt = seq // tl
    n_steps = bsz * nt
    vec = pl.BlockSpec((1, ch), lambda b, t: (0, 0))
    for w in cast_weights:
        assert w.shape[0] % (n_steps * 16) == 0, w.shape
    cast_specs = [pl.BlockSpec((w.shape[0] // n_steps, w.shape[1]), lambda b, t: (b * nt + t, 0))
                  for w in cast_weights]
    outs = pl.pallas_call(
        functools.partial(_conv_kernel, rb=rb, n_cast=len(cast_weights)),
        grid=(bsz, nt),
        in_specs=[pl.BlockSpec((1, tl, ch), lambda b, t: (b, t, 0)),
                  pl.BlockSpec((1, CONV_HALO, ch), lambda b, t: (b, jnp.maximum(t * hb - 1, 0), 0)),
                  pl.BlockSpec((1, CONV_HALO, ch), lambda b, t: (b, jnp.minimum((t + 1) * hb, n_halo - 1), 0)),
                  pl.BlockSpec((CONV_K, ch), lambda b, t: (0, 0)),
                  vec, vec, vec] + cast_specs,
        out_specs=[pl.BlockSpec((1, tl, ch), lambda b, t: (b, t, 0))] + cast_specs,
        out_shape=[jax.ShapeDtypeStruct((bsz, seq, ch), BF16)]
                  + [jax.ShapeDtypeStruct(w.shape, BF16) for w in cast_weights],
        scratch_shapes=[pltpu.VMEM((tl + 2 * CONV_HALO, ch), BF16), pltpu.VMEM((tl, ch), F32)],
        compiler_params=_params(2),
        name="conv",
    )(z3, z3, z3, conv_w, conv_b, ln_w, ln_b, *cast_weights)
    return outs[0], outs[1:]


class _Scan:
    def __init__(self, q_ref, v_ref, g_ref, o_ref, st, scr, backward):
        self.q_ref, self.v_ref, self.g_ref, self.o_ref, self.st = q_ref, v_ref, g_ref, o_ref, st
        self.kd_s, self.qa_s, self.ka_s, self.qe_s, self.dec_s, self.kv_s, self.sb_s = scr
        rows, width = q_ref.shape[1], q_ref.shape[2]
        self.n_chunks, self.n_heads = rows // REC_CHUNK, width // REC_DK
        r_i = lax.broadcasted_iota(jnp.int32, (rows, rows), 0)
        c_i = lax.broadcasted_iota(jnp.int32, (rows, rows), 1)
        same_chunk = (r_i // REC_CHUNK) == (c_i // REC_CHUNK)
        tri = (same_chunk & ((c_i >= r_i) if backward else (c_i <= r_i))).astype(BF16)
        m_r = lax.broadcasted_iota(jnp.int32, (REC_CHUNK, REC_CHUNK), 0)
        m_c = lax.broadcasted_iota(jnp.int32, (REC_CHUNK, REC_CHUNK), 1)
        self.mask = (m_c >= m_r) if backward else (m_c <= m_r)
        ref_f = REC_CHUNK // 2 - 1
        self.last, self.ref = (0, REC_CHUNK - 1 - ref_f) if backward else (REC_CHUNK - 1, ref_f)
        self.order = list(range(self.n_chunks - 1, -1, -1) if backward else range(self.n_chunks))
        self.b = _tri_sum(tri, g_ref[0])
        self.scores = {}

    def _rows(self, c):
        return slice(c * REC_CHUNK, (c + 1) * REC_CHUNK)

    def _heads(self):
        return [(h, slice(h * REC_DK, (h + 1) * REC_DK)) for h in range(self.n_heads)]

    def prep(self, c):
        rs = self._rows(c)
        bc, qc = self.b[rs], self.q_ref[0, rs, :].astype(F32)
        b_last = bc[self.last:self.last + 1, :]
        b_ref = bc[self.ref:self.ref + 1, :]
        k = 1.0 - jnp.exp2(self.g_ref[0, rs, :].astype(F32))
        self.kd_s[rs, :] = (k * jnp.exp2(b_last - bc)).astype(BF16)
        self.qa_s[rs, :] = (qc * jnp.exp2(bc - b_ref)).astype(BF16)
        self.ka_s[rs, :] = (k * jnp.exp2(b_ref - bc)).astype(BF16)
        self.qe_s[rs, :] = (qc * jnp.exp2(bc)).astype(BF16)
        self.dec_s[c:c + 1, :] = jnp.exp2(b_last)

    def intra(self, c):
        rs = self._rows(c)
        for h, ls in self._heads():
            s = _dot_nt(self.qa_s[rs, ls], self.ka_s[rs, ls])
            self.scores[c, h] = jnp.where(self.mask, s, 0.0).astype(BF16)
            self.kv_s[c, h] = _dot_tn(self.v_ref[0, rs, ls], self.kd_s[rs, ls])

    def recur(self):
        for h, ls in self._heads():
            s = self.st[h]
            for c in self.order:
                self.sb_s[c, h] = s.astype(BF16)
                s = self.dec_s[c:c + 1, ls] * s + self.kv_s[c, h]
            self.st[h] = s

    def out(self, c):
        rs = self._rows(c)
        for h, ls in self._heads():
            self.o_ref[0, rs, ls] = (_dot(self.scores[c, h], self.v_ref[0, rs, ls])
                                     + _dot_nt(self.qe_s[rs, ls], self.sb_s[c, h]))


def _gla_kernel(qf_ref, vf_ref, gf_ref, qb_ref, vb_ref, gb_ref, s0f_ref, s0b_ref,
                of_ref, ob_ref, stf, stb, *scr):
    @pl.when(pl.program_id(1) == 0)
    def _():
        stf[...] = s0f_ref[0]
        stb[...] = s0b_ref[0]

    half = len(scr) // 2
    fwd = _Scan(qf_ref, vf_ref, gf_ref, of_ref, stf, scr[:half], backward=False)
    bwd = _Scan(qb_ref, vb_ref, gb_ref, ob_ref, stb, scr[half:], backward=True)
    n = fwd.n_chunks
    for i in range(n):
        fwd.prep(fwd.order[i])
        if i > 0:
            bwd.intra(bwd.order[i - 1])
        bwd.prep(bwd.order[i])
        fwd.intra(fwd.order[i])
    bwd.intra(bwd.order[n - 1])
    fwd.recur()
    bwd.recur()
    for i in range(n):
        fwd.out(fwd.order[i])
        bwd.out(bwd.order[i])


def _gla(z3, s0f, s0b, rows, width):
    bsz, seq, _ = z3.shape
    nb = seq // rows
    nh = width // REC_DK
    nc = rows // REC_CHUNK
    state = pltpu.VMEM((nh, REC_DK, REC_DK), F32)
    per_direction = ([pltpu.VMEM((rows, width), BF16)] * 4
                     + [pltpu.VMEM((max(nc, SUBLANES), width), F32),
                        pltpu.VMEM((nc, nh, REC_DK, REC_DK), F32),
                        pltpu.VMEM((nc, nh, REC_DK, REC_DK), BF16)])
    fwd = lambda col: pl.BlockSpec((1, rows, width), lambda b, n, col=col: (b, n, col))
    bwd = lambda col: pl.BlockSpec((1, rows, width), lambda b, n, col=col: (b, nb - 1 - n, col))
    st_spec = pl.BlockSpec((1, nh, REC_DK, REC_DK), lambda b, n: (b, 0, 0, 0))
    o_shape = jax.ShapeDtypeStruct((bsz, seq, width), F32)
    return pl.pallas_call(
        _gla_kernel,
        grid=(bsz, nb),
        in_specs=[fwd(1), fwd(3), fwd(4), bwd(1), bwd(3), bwd(5), st_spec, st_spec],
        out_specs=[pl.BlockSpec((1, rows, width), lambda b, n: (b, n, 0)),
                   pl.BlockSpec((1, rows, width), lambda b, n: (b, nb - 1 - n, 0))],
        out_shape=[o_shape, o_shape],
        scratch_shapes=[state, state] + per_direction * 2,
        compiler_params=_params(2),
        name="gla",
    )(z3, z3, z3, z3, z3, z3, s0f, s0b)


def _outproj_kernel(u_ref, of_ref, ob_ref, gs_ref, x_ref, mod_ref, rnw_ref, postw_ref, prew_ref, w_ref,
                    x1_ref, h2_ref):
    gw = u_ref.shape[1]
    for m0 in range(0, u_ref.shape[0], SUB_M):
        rs = slice(m0, m0 + SUB_M)
        o = of_ref[rs, :] + ob_ref[rs, :]
        heads = []
        for h in range(gw // REC_DK):
            oh = o[:, h * REC_DK:(h + 1) * REC_DK]
            heads.append(oh * lax.rsqrt(jnp.mean(oh * oh, axis=-1, keepdims=True) + NORM_EPS))
        on = jnp.concatenate(heads, axis=-1) * rnw_ref[...] * gs_ref[rs, :]
        y = _dot(u_ref[rs, :].astype(BF16), w_ref[0:gw, :]) + _dot(on.astype(BF16), w_ref[gw:2 * gw, :])
        x1 = x_ref[rs, :] + mod_ref[0, 2:3, :] * _rms_norm(y, postw_ref[...])
        x1_ref[rs, :] = x1
        h2 = _rms_norm(x1, prew_ref[...])
        h2_ref[rs, :] = (h2 * (1.0 + mod_ref[0, 4:5, :]) + mod_ref[0, 3:4, :]).astype(BF16)


def _outproj(uconv, o_f, o_b, z2, x2, mod3, rnw, postw, prew, w_out_bf, tm, rows_per_batch):
    m, d = x2.shape
    gw = d // 2
    tiles_per_batch = rows_per_batch // tm
    half = pl.BlockSpec((tm, gw), lambda i: (i, 0))
    full = pl.BlockSpec((tm, d), lambda i: (i, 0))
    vec = lambda n: pl.BlockSpec((1, n), lambda i: (0, 0))
    return pl.pallas_call(
        _outproj_kernel,
        grid=(m // tm,),
        in_specs=[half, half, half,
                  pl.BlockSpec((tm, gw), lambda i: (i, 2)),
                  full,
                  pl.BlockSpec((1,) + mod3.shape[1:], lambda i: (i // tiles_per_batch, 0, 0)),
                  vec(gw), vec(d), vec(d),
                  pl.BlockSpec((d, d), lambda i: (0, 0))],
        out_specs=[full, full],
        out_shape=[jax.ShapeDtypeStruct((m, d), F32), jax.ShapeDtypeStruct((m, d), BF16)],
        compiler_params=_params(1),
        name="outproj",
    )(uconv, o_f, o_b, z2, x2, mod3, rnw, postw, prew, w_out_bf)


def _mlp_kernel(h_ref, x1_ref, mod_ref, postw_ref, wu_ref, wd_ref, o_ref, acc):
    k = pl.program_id(1)
    last = pl.num_programs(1) - 1

    def partial_sum(rs):
        hid = jnp.maximum(_dot(h_ref[rs, :], wu_ref[...]), 0.0)
        return _dot((hid * hid).astype(BF16), wd_ref[...])

    @pl.when(k == 0)
    def _():
        acc[...] = partial_sum(slice(None))

    @pl.when((k > 0) & (k < last))
    def _():
        acc[...] += partial_sum(slice(None))

    @pl.when(k == last)
    def _():
        for m0 in range(0, acc.shape[0], SUB_M):
            rs = slice(m0, m0 + SUB_M)
            y = acc[rs, :] + partial_sum(rs)
            o_ref[rs, :] = x1_ref[rs, :] + mod_ref[0, 5:6, :] * _rms_norm(y, postw_ref[...])


def _mlp(h2, x1, mod3, postw, w_up_bf, w_down_bf, tm, tf, rows_per_batch):
    m, d = x1.shape
    ff = w_up_bf.shape[1]
    assert ff // tf >= 2, "the first and last hidden chunks take different branches"
    tiles_per_batch = rows_per_batch // tm
    return pl.pallas_call(
        _mlp_kernel,
        grid=(m // tm, ff // tf),
        in_specs=[pl.BlockSpec((tm, d), lambda i, k: (i, 0)),
                  pl.BlockSpec((tm, d), lambda i, k: (i, 0)),
                  pl.BlockSpec((1,) + mod3.shape[1:], lambda i, k: (i // tiles_per_batch, 0, 0)),
                  pl.BlockSpec((1, d), lambda i, k: (0, 0)),
                  pl.BlockSpec((d, tf), lambda i, k: (0, k)),
                  pl.BlockSpec((tf, d), lambda i, k: (k, 0))],
        out_specs=pl.BlockSpec((tm, d), lambda i, k: (i, 0)),
        out_shape=jax.ShapeDtypeStruct((m, d), F32),
        scratch_shapes=[pltpu.VMEM((tm, d), F32)],
        compiler_params=_params(2),
        name="mlp",
    )(h2, x1, mod3, postw, w_up_bf, w_down_bf)


def _tile(n, target):
    t = min(n, target)
    assert n % t == 0, (n, t)
    return t


def kernel(x, c, ctx, c_ctx, w_ada, b_ada, mix_pre_w, mix_post_w, mlp_pre_w, mlp_post_w, w_in, conv_w, conv_b,
           conv_ln_w, conv_ln_b, rec_lb_logits, rec_norm_w, w_out, w_up, w_down):
    bsz, seq, d = x.shape
    assert w_in.shape[0] == 1, "single-layer block"
    gw = d // 2
    assert gw % REC_DK == 0 and seq % REC_CHUNK == 0 and ctx.shape[1] % SUBLANES == 0
    m = bsz * seq

    cc = jnp.concatenate([c, c_ctx[None, :], jnp.zeros((SUBLANES - bsz - 1, d), c.dtype)], axis=0)
    mod = _ada(cc, w_ada[0], b_ada, _tile(w_ada.shape[2], 1024))
    mod3 = mod.reshape(SUBLANES, 6, d)

    lbl = rec_lb_logits.astype(F32)

    s0f, s0b = _ctx_states(ctx, mod3, mix_pre_w, lbl, w_in[0], col0=2 * gw + 2 * gw)

    x2 = x.reshape(m, d)
    tm = _tile(seq, 512)
    z2 = _inproj(x2, mod3, mix_pre_w, lbl, w_in[0], _tile(seq, 1024), _tile(gw, 512), seq)
    z3 = z2.reshape(bsz, seq, 6 * gw)

    tl = _tile(seq, 512)
    uconv, (w_out_bf, w_up_bf, w_down_bf) = _conv(z3, conv_w[0], conv_b, conv_ln_w, conv_ln_b, tl, _tile(tl, 64),
                                                  [w_out[0], w_up[0], w_down[0]])
    o_f, o_b = _gla(z3, s0f, s0b, _tile(seq, 256), gw)

    x1, h2 = _outproj(uconv.reshape(m, gw), o_f.reshape(m, gw), o_b.reshape(m, gw), z2, x2, mod3,
                      rec_norm_w, mix_post_w, mlp_pre_w, w_out_bf, _tile(seq, 512), seq)
    out = _mlp(h2, x1, mod3, mlp_post_w, w_up_bf, w_down_bf, tm, _tile(w_up.shape[2], 1024), seq)
    return out.reshape(bsz, seq, d)
```

```python
import functools

import jax
import jax.numpy as jnp
from jax import lax
from jax.experimental import pallas as pl
from jax.experimental.pallas import tpu as pltpu

NORM_EPS = 1e-6
REC_DK = 128
REC_CHUNK = 64
CONV_K = 31
CONV_HALO = 16
CONV_SLAB = 256
SUBLANES = 8
SUB_M = 256
VMEM_LIMIT_BYTES = 56 * 1024 * 1024

F32 = jnp.float32
BF16 = jnp.bfloat16


def _params(n_axes):
    return pltpu.CompilerParams(dimension_semantics=("arbitrary",) * n_axes,
                                vmem_limit_bytes=VMEM_LIMIT_BYTES)


def _sigmoid(x):
    return 1.0 / (1.0 + jnp.exp(-x))


def _rms_norm(xf, w):
    ms = jnp.mean(xf * xf, axis=-1, keepdims=True)
    return xf * lax.rsqrt(ms + NORM_EPS) * w


def _dot(a, b):
    return jnp.dot(a, b, preferred_element_type=F32)


def _dot_nt(a, b):
    return lax.dot_general(a, b, (((1,), (1,)), ((), ())), preferred_element_type=F32)


def _dot_tn(a, b):
    return lax.dot_general(a, b, (((0,), (0,)), ((), ())), preferred_element_type=F32)


def _tri_sum(tri, g):
    if g.dtype == BF16:
        return _dot(tri, g)
    g_hi = g.astype(BF16)
    g_lo = (g - g_hi.astype(F32)).astype(BF16)
    return _dot(tri, g_hi) + _dot(tri, g_lo)


def _lower_bound(lbl, direction):
    m = jnp.max(lbl, axis=0)
    e = jnp.exp(lbl - m[None])
    sm0 = e[0] / jnp.sum(e, axis=0)
    return sm0[direction:direction + 1]


def _log_forget(z, lb):
    return jnp.log2(lb + (1.0 - lb) * _sigmoid(z))


def _ada_kernel(c_ref, w_ref, b_ref, o_ref):
    c = c_ref[...]
    s = (c * _sigmoid(c)).astype(BF16)
    o_ref[...] = _dot(s, w_ref[...].astype(BF16)) + b_ref[...]


def _ada(cc, w_ada, b_ada, tn, n_cols):
    rows, d = cc.shape
    return pl.pallas_call(
        _ada_kernel,
        grid=(n_cols // tn,),
        in_specs=[pl.BlockSpec((rows, d), lambda j: (0, 0)),
                  pl.BlockSpec((d, tn), lambda j: (0, j)),
                  pl.BlockSpec((1, tn), lambda j: (0, j))],
        out_specs=pl.BlockSpec((rows, tn), lambda j: (0, j)),
        out_shape=jax.ShapeDtypeStruct((rows, n_cols), F32),
        compiler_params=_params(1),
        name="ada",
    )(cc, w_ada, b_ada)


def _ctx_kernel(ctx_ref, mod_ref, npw_ref, lbl_ref, wi_ref, wf_ref, wb_ref, sf_ref, sb_ref, h_scr):
    @pl.when(pl.program_id(1) == 0)
    def _():
        h = _rms_norm(ctx_ref[0], npw_ref[...])
        h_scr[...] = (h * (1.0 + mod_ref[0, 1:2, :]) + mod_ref[0, 0:1, :]).astype(BF16)

    hb = h_scr[...]
    lc = hb.shape[0]
    v = _dot(hb, wi_ref[...].astype(BF16)).astype(BF16)
    lbl = lbl_ref[...]
    g_f = _log_forget(_dot(hb, wf_ref[...].astype(BF16)), _lower_bound(lbl, 0))
    g_b = _log_forget(_dot(hb, wb_ref[...].astype(BF16)), _lower_bound(lbl, 1))
    row = lax.broadcasted_iota(jnp.int32, (lc, lc), 0)
    col = lax.broadcasted_iota(jnp.int32, (lc, lc), 1)
    incl = (col <= row).astype(BF16)
    excl = (col < row).astype(BF16)
    b_f = _tri_sum(incl, g_f)
    kd_f = ((1.0 - jnp.exp2(g_f)) * jnp.exp2(b_f[lc - 1:lc, :] - b_f)).astype(BF16)
    c_b = _tri_sum(excl, g_b)
    kd_b = ((1.0 - jnp.exp2(g_b)) * jnp.exp2(c_b)).astype(BF16)
    for h in range(v.shape[1] // REC_DK):
        ls = slice(h * REC_DK, (h + 1) * REC_DK)
        sf_ref[0, h] = _dot_tn(v[:, ls], kd_f[:, ls])
        sb_ref[0, h] = _dot_tn(v[:, ls], kd_b[:, ls])


def _ctx_states(ctx, mod3, npw, lbl, w_in, col0, heads_per_step):
    bsz, lc, d = ctx.shape
    nh = lbl.shape[-1] // REC_DK
    hg = heads_per_step
    wcols = hg * REC_DK
    cb = col0 // wcols
    ng = nh // hg
    wspec = lambda off: pl.BlockSpec((d, wcols), lambda b, g, off=off: (0, cb + off * ng + g))
    st_spec = pl.BlockSpec((1, hg, REC_DK, REC_DK), lambda b, g: (b, g, 0, 0))
    st_shape = jax.ShapeDtypeStruct((bsz, nh, REC_DK, REC_DK), F32)
    return pl.pallas_call(
        _ctx_kernel,
        grid=(bsz, ng),
        in_specs=[pl.BlockSpec((1, lc, d), lambda b, g: (b, 0, 0)),
                  pl.BlockSpec((1,) + mod3.shape[1:], lambda b, g: (bsz, 0, 0)),
                  pl.BlockSpec((1, d), lambda b, g: (0, 0)),
                  pl.BlockSpec(lbl.shape[:2] + (wcols,), lambda b, g: (0, 0, g)),
                  wspec(0), wspec(1), wspec(2)],
        out_specs=[st_spec, st_spec],
        out_shape=[st_shape, st_shape],
        scratch_shapes=[pltpu.VMEM((lc, d), BF16)],
        compiler_params=_params(2),
        name="ctx_states",
    )(ctx, mod3, npw, lbl, w_in, w_in, w_in)


def _project_tiles(h_scr, w_ref, wb_scr, tn, epilogue):
    for m0 in range(0, h_scr.shape[0], SUB_M):
        rs = slice(m0, m0 + SUB_M)
        for n0 in range(0, w_ref.shape[1], tn):
            cs = slice(n0, n0 + tn)
            if m0 == 0:
                wb_scr[:, cs] = w_ref[:, cs].astype(BF16)
            epilogue(rs, cs, _dot(h_scr[rs, :], wb_scr[:, cs]))


def _inproj_kernel(x_ref, mod_ref, npw_ref, lbl_ref, w_ref, o_ref, h_scr, a_scr, wb_scr, *, tn):
    j = pl.program_id(1)

    @pl.when(j == 0)
    def _():
        h = _rms_norm(x_ref[...], npw_ref[...])
        h_scr[...] = (h * (1.0 + mod_ref[0, 1:2, :]) + mod_ref[0, 0:1, :]).astype(BF16)

        def park(rs, cs, z):
            a_scr[rs, cs] = z
        _project_tiles(h_scr, w_ref, wb_scr, tn, park)

    @pl.when(j == 1)
    def _():
        def glu(rs, cs, z):
            o_ref[rs, cs] = (a_scr[rs, cs] * _sigmoid(z)).astype(o_ref.dtype)
        _project_tiles(h_scr, w_ref, wb_scr, tn, glu)

    @pl.when((j == 2) | (j == 3))
    def _():
        def swish(rs, cs, z):
            o_ref[rs, cs] = (z * _sigmoid(z)).astype(o_ref.dtype)
        _project_tiles(h_scr, w_ref, wb_scr, tn, swish)

    @pl.when(j == 4)
    def _():
        def values(rs, cs, z):
            o_ref[rs, cs] = z.astype(o_ref.dtype)
        _project_tiles(h_scr, w_ref, wb_scr, tn, values)

    @pl.when(j >= 5)
    def _():
        lbl = lbl_ref[...]
        lb = jnp.where(j == 5, _lower_bound(lbl, 0), _lower_bound(lbl, 1))

        def log_gate(rs, cs, z):
            o_ref[rs, cs] = _log_forget(z, lb[:, cs]).astype(o_ref.dtype)
        _project_tiles(h_scr, w_ref, wb_scr, tn, log_gate)


def _inproj(x2, mod3, npw, lbl, w_in, tm, tn, rows_per_batch):
    m, d = x2.shape
    gw = d // 2
    n_groups = w_in.shape[1] // gw
    tiles_per_batch = rows_per_batch // tm
    return pl.pallas_call(
        functools.partial(_inproj_kernel, tn=tn),
        grid=(m // tm, n_groups),
        in_specs=[pl.BlockSpec((tm, d), lambda i, j: (i, 0)),
                  pl.BlockSpec((1,) + mod3.shape[1:], lambda i, j: (i // tiles_per_batch, 0, 0)),
                  pl.BlockSpec((1, d), lambda i, j: (0, 0)),
                  pl.BlockSpec(lbl.shape, lambda i, j: (0, 0, 0)),
                  pl.BlockSpec((d, gw), lambda i, j: (0, j))],
        out_specs=pl.BlockSpec((tm, gw), lambda i, j: (i, jnp.maximum(j - 1, 0))),
        out_shape=jax.ShapeDtypeStruct((m, (n_groups - 1) * gw), BF16),
        scratch_shapes=[pltpu.VMEM((tm, d), BF16), pltpu.VMEM((tm, gw), F32), pltpu.VMEM((d, gw), BF16)],
        compiler_params=_params(2),
        name="inproj",
    )(x2, mod3, npw, lbl, w_in)


def _conv_kernel(u_ref, ul_ref, ur_ref, cw_ref, cb_ref, lnw_ref, lnb_ref, cc_ref, wada_ref, bada_ref, *rest,
                 rb, n_cast):
    cast_in, o_ref, modb_ref = rest[:n_cast], rest[n_cast], rest[n_cast + 1]
    cast_out = rest[n_cast + 2:2 * n_cast + 2]
    buf, y_scr = rest[2 * n_cast + 2:]
    for src, dst in zip(cast_in, cast_out):
        dst[...] = src[...].astype(dst.dtype)
    _ada_kernel(cc_ref, wada_ref, bada_ref, modb_ref)

    t = pl.program_id(1)
    tl, ch = u_ref.shape[1], u_ref.shape[2]
    buf[pl.ds(CONV_HALO, tl), :] = u_ref[0]
    buf[pl.ds(0, CONV_HALO), :] = jnp.where(t > 0, ul_ref[0], jnp.zeros_like(ul_ref[0]))
    buf[pl.ds(CONV_HALO + tl, CONV_HALO), :] = jnp.where(t < pl.num_programs(1) - 1, ur_ref[0],
                                                          jnp.zeros_like(ur_ref[0]))

    shift0 = CONV_HALO - CONV_K // 2
    win = rb + 2 * CONV_HALO
    span = win - SUBLANES
    r_i = lax.broadcasted_iota(jnp.int32, (span, win), 0)
    c_i = lax.broadcasted_iota(jnp.int32, (span, win), 1)
    shifts = [(c_i == r_i + p).astype(BF16) for p in range(SUBLANES)]

    def row_block(i, carry):
        r0 = pl.multiple_of(i * rb, rb)

        for l0 in range(0, ch, CONV_SLAB):
            ls = slice(l0, l0 + CONV_SLAB)
            w = buf[pl.ds(r0, win), ls]
            acc = jnp.zeros((rb, CONV_SLAB), F32)
            for p in range(SUBLANES):
                sp = _dot(shifts[p], w)
                for k in range(CONV_K):
                    if (k + shift0) % SUBLANES == p:
                        a0 = (k + shift0) // SUBLANES * SUBLANES
                        acc = acc + sp[a0:a0 + rb, :] * cw_ref[k:k + 1, ls]
            y_scr[pl.ds(r0, rb), ls] = acc
        return carry

    lax.fori_loop(0, tl // rb, row_block, 0)

    y = y_scr[...] + cb_ref[...]
    mu = jnp.mean(y, axis=-1, keepdims=True)
    yc = y - mu
    var = jnp.mean(yc * yc, axis=-1, keepdims=True)
    yn = yc * lax.rsqrt(var + NORM_EPS) * lnw_ref[...] + lnb_ref[...]
    o_ref[0] = (yn * _sigmoid(yn)).astype(o_ref.dtype)


def _conv(z3, conv_w, conv_b, ln_w, ln_b, tl, rb, cast_weights, cc, w_ada, b_ada, ada_col0):
    bsz, seq, _ = z3.shape
    ch = conv_w.shape[1]
    hb = tl // CONV_HALO
    n_halo = seq // CONV_HALO
    nt = seq // tl
    n_steps = bsz * nt
    vec = pl.BlockSpec((1, ch), lambda b, t: (0, 0))
    for w in cast_weights:
        assert w.shape[0] % (n_steps * 16) == 0, w.shape
    cast_specs = [pl.BlockSpec((w.shape[0] // n_steps, w.shape[1]), lambda b, t: (b * nt + t, 0))
                  for w in cast_weights]
    n_ada = w_ada.shape[1] - ada_col0
    ta = n_ada // n_steps
    assert n_ada % n_steps == 0 and ta % 128 == 0 and ada_col0 % ta == 0
    ada_specs = [pl.BlockSpec(cc.shape, lambda b, t: (0, 0)),
                 pl.BlockSpec((w_ada.shape[0], ta), lambda b, t: (0, ada_col0 // ta + b * nt + t)),
                 pl.BlockSpec((1, ta), lambda b, t: (0, ada_col0 // ta + b * nt + t))]
    outs = pl.pallas_call(
        functools.partial(_conv_kernel, rb=rb, n_cast=len(cast_weights)),
        grid=(bsz, nt),
        in_specs=[pl.BlockSpec((1, tl, ch), lambda b, t: (b, t, 0)),
                  pl.BlockSpec((1, CONV_HALO, ch), lambda b, t: (b, jnp.maximum(t * hb - 1, 0), 0)),
                  pl.BlockSpec((1, CONV_HALO, ch), lambda b, t: (b, jnp.minimum((t + 1) * hb, n_halo - 1), 0)),
                  pl.BlockSpec((CONV_K, ch), lambda b, t: (0, 0)),
                  vec, vec, vec] + ada_specs + cast_specs,
        out_specs=[pl.BlockSpec((1, tl, ch), lambda b, t: (b, t, 0)),
                   pl.BlockSpec((cc.shape[0], ta), lambda b, t: (0, b * nt + t))] + cast_specs,
        out_shape=[jax.ShapeDtypeStruct((bsz, seq, ch), BF16), jax.ShapeDtypeStruct((cc.shape[0], n_ada), F32)]
                  + [jax.ShapeDtypeStruct(w.shape, BF16) for w in cast_weights],
        scratch_shapes=[pltpu.VMEM((tl + 2 * CONV_HALO, ch), BF16), pltpu.VMEM((tl, ch), F32)],
        compiler_params=_params(2),
        name="conv",
    )(z3, z3, z3, conv_w, conv_b, ln_w, ln_b, cc, w_ada, b_ada, *cast_weights)
    return outs[0], outs[1], outs[2:]


class _Scan:
    def __init__(self, q_ref, v_ref, g_ref, o_ref, st, scr, backward):
        self.q_ref, self.v_ref, self.g_ref, self.o_ref, self.st = q_ref, v_ref, g_ref, o_ref, st
        self.kd_s, self.qa_s, self.ka_s, self.qe_s, self.dec_s, self.kv_s, self.sb_s = scr
        rows, width = q_ref.shape[1], q_ref.shape[2]
        self.n_chunks, self.n_heads = rows // REC_CHUNK, width // REC_DK
        r_i = lax.broadcasted_iota(jnp.int32, (rows, rows), 0)
        c_i = lax.broadcasted_iota(jnp.int32, (rows, rows), 1)
        same_chunk = (r_i // REC_CHUNK) == (c_i // REC_CHUNK)
        tri = (same_chunk & ((c_i >= r_i) if backward else (c_i <= r_i))).astype(BF16)
        m_r = lax.broadcasted_iota(jnp.int32, (REC_CHUNK, REC_CHUNK), 0)
        m_c = lax.broadcasted_iota(jnp.int32, (REC_CHUNK, REC_CHUNK), 1)
        self.mask = (m_c >= m_r) if backward else (m_c <= m_r)
        ref_f = REC_CHUNK // 2 - 1
        self.last, self.ref = (0, REC_CHUNK - 1 - ref_f) if backward else (REC_CHUNK - 1, ref_f)
        self.order = list(range(self.n_chunks - 1, -1, -1) if backward else range(self.n_chunks))
        self.b = _tri_sum(tri, g_ref[0])
        self.scores = {}

    def _rows(self, c):
        return slice(c * REC_CHUNK, (c + 1) * REC_CHUNK)

    def _heads(self):
        return [(h, slice(h * REC_DK, (h + 1) * REC_DK)) for h in range(self.n_heads)]

    def prep(self, c):
        rs = self._rows(c)
        bc, qc = self.b[rs], self.q_ref[0, rs, :].astype(F32)
        b_last = bc[self.last:self.last + 1, :]
        b_ref = bc[self.ref:self.ref + 1, :]
        k = 1.0 - jnp.exp2(self.g_ref[0, rs, :].astype(F32))
        self.kd_s[rs, :] = (k * jnp.exp2(b_last - bc)).astype(BF16)
        self.qa_s[rs, :] = (qc * jnp.exp2(bc - b_ref)).astype(BF16)
        self.ka_s[rs, :] = (k * jnp.exp2(b_ref - bc)).astype(BF16)
        self.qe_s[rs, :] = (qc * jnp.exp2(bc)).astype(BF16)
        self.dec_s[c:c + 1, :] = jnp.exp2(b_last)

    def intra(self, c):
        rs = self._rows(c)
        for h, ls in self._heads():
            s = _dot_nt(self.qa_s[rs, ls], self.ka_s[rs, ls])
            self.scores[c, h] = jnp.where(self.mask, s, 0.0).astype(BF16)
            self.kv_s[c, h] = _dot_tn(self.v_ref[0, rs, ls], self.kd_s[rs, ls])

    def recur(self):
        for h, ls in self._heads():
            s = self.st[h]
            for c in self.order:
                self.sb_s[c, h] = s.astype(BF16)
                s = self.dec_s[c:c + 1, ls] * s + self.kv_s[c, h]
            self.st[h] = s

    def out(self, c):
        rs = self._rows(c)
        for h, ls in self._heads():
            self.o_ref[0, rs, ls] = (_dot(self.scores[c, h], self.v_ref[0, rs, ls])
                                     + _dot_nt(self.qe_s[rs, ls], self.sb_s[c, h]))


def _gla_kernel(qf_ref, vf_ref, gf_ref, qb_ref, vb_ref, gb_ref, s0f_ref, s0b_ref,
                of_ref, ob_ref, stf, stb, *scr):
    @pl.when(pl.program_id(1) == 0)
    def _():
        stf[...] = s0f_ref[0]
        stb[...] = s0b_ref[0]

    half = len(scr) // 2
    fwd = _Scan(qf_ref, vf_ref, gf_ref, of_ref, stf, scr[:half], backward=False)
    bwd = _Scan(qb_ref, vb_ref, gb_ref, ob_ref, stb, scr[half:], backward=True)
    n = fwd.n_chunks
    for i in range(n):
        fwd.prep(fwd.order[i])
        if i > 0:
            bwd.intra(bwd.order[i - 1])
        bwd.prep(bwd.order[i])
        fwd.intra(fwd.order[i])
    bwd.intra(bwd.order[n - 1])
    fwd.recur()
    bwd.recur()
    for i in range(n):
        fwd.out(fwd.order[i])
        bwd.out(bwd.order[i])


def _gla(z3, s0f, s0b, rows, width):
    bsz, seq, _ = z3.shape
    nb = seq // rows
    nh = width // REC_DK
    nc = rows // REC_CHUNK
    state = pltpu.VMEM((nh, REC_DK, REC_DK), F32)
    per_direction = ([pltpu.VMEM((rows, width), BF16)] * 4
                     + [pltpu.VMEM((max(nc, SUBLANES), width), F32),
                        pltpu.VMEM((nc, nh, REC_DK, REC_DK), F32),
                        pltpu.VMEM((nc, nh, REC_DK, REC_DK), BF16)])
    fwd = lambda col: pl.BlockSpec((1, rows, width), lambda b, n, col=col: (b, n, col))
    bwd = lambda col: pl.BlockSpec((1, rows, width), lambda b, n, col=col: (b, nb - 1 - n, col))
    st_spec = pl.BlockSpec((1, nh, REC_DK, REC_DK), lambda b, n: (b, 0, 0, 0))
    o_shape = jax.ShapeDtypeStruct((bsz, seq, width), F32)
    return pl.pallas_call(
        _gla_kernel,
        grid=(bsz, nb),
        in_specs=[fwd(1), fwd(3), fwd(4), bwd(1), bwd(3), bwd(5), st_spec, st_spec],
        out_specs=[pl.BlockSpec((1, rows, width), lambda b, n: (b, n, 0)),
                   pl.BlockSpec((1, rows, width), lambda b, n: (b, nb - 1 - n, 0))],
        out_shape=[o_shape, o_shape],
        scratch_shapes=[state, state] + per_direction * 2,
        compiler_params=_params(2),
        name="gla",
    )(z3, z3, z3, z3, z3, z3, s0f, s0b)


def _outproj_kernel(u_ref, of_ref, ob_ref, gs_ref, x_ref, mod_ref, rnw_ref, postw_ref, w_ref, x1_ref):
    gw = u_ref.shape[1]
    for m0 in range(0, u_ref.shape[0], SUB_M):
        rs = slice(m0, m0 + SUB_M)
        o = of_ref[rs, :] + ob_ref[rs, :]
        heads = []
        for h in range(gw // REC_DK):
            oh = o[:, h * REC_DK:(h + 1) * REC_DK]
            heads.append(oh * lax.rsqrt(jnp.mean(oh * oh, axis=-1, keepdims=True) + NORM_EPS))
        on = jnp.concatenate(heads, axis=-1) * rnw_ref[...] * gs_ref[rs, :]
        y = _dot(u_ref[rs, :].astype(BF16), w_ref[0:gw, :]) + _dot(on.astype(BF16), w_ref[gw:2 * gw, :])
        x1_ref[rs, :] = x_ref[rs, :] + mod_ref[0, 0:1, :] * _rms_norm(y, postw_ref[...])


def _outproj(uconv, o_f, o_b, z2, x2, mod3, rnw, postw, w_out_bf, tm, rows_per_batch):
    m, d = x2.shape
    gw = d // 2
    tiles_per_batch = rows_per_batch // tm
    half = pl.BlockSpec((tm, gw), lambda i: (i, 0))
    full = pl.BlockSpec((tm, d), lambda i: (i, 0))
    vec = lambda n: pl.BlockSpec((1, n), lambda i: (0, 0))
    return pl.pallas_call(
        _outproj_kernel,
        grid=(m // tm,),
        in_specs=[half, half, half,
                  pl.BlockSpec((tm, gw), lambda i: (i, 2)),
                  full,
                  pl.BlockSpec((1,) + mod3.shape[1:], lambda i: (i // tiles_per_batch, 0, 0)),
                  vec(gw), vec(d),
                  pl.BlockSpec((d, d), lambda i: (0, 0))],
        out_specs=full,
        out_shape=jax.ShapeDtypeStruct((m, d), F32),
        compiler_params=_params(1),
        name="outproj",
    )(uconv, o_f, o_b, z2, x2, mod3, rnw, postw, w_out_bf)


def _mlp_kernel(x1_ref, mod_ref, prew_ref, postw_ref, wu_ref, wd_ref, o_ref, h_scr):
    k = pl.program_id(1)
    last = pl.num_programs(1) - 1

    def partial_sum(rs):
        hid = jnp.maximum(_dot(h_scr[rs, :], wu_ref[...]), 0.0)
        return _dot((hid * hid).astype(BF16), wd_ref[...])

    @pl.when(k == 0)
    def _():
        for m0 in range(0, h_scr.shape[0], SUB_M):
            rs = slice(m0, m0 + SUB_M)
            h = _rms_norm(x1_ref[rs, :], prew_ref[...])
            h_scr[rs, :] = (h * (1.0 + mod_ref[0, 2:3, :]) + mod_ref[0, 1:2, :]).astype(BF16)
            o_ref[rs, :] = partial_sum(rs)

    @pl.when((k > 0) & (k < last))
    def _():
        o_ref[...] += partial_sum(slice(None))

    @pl.when(k == last)
    def _():
        for m0 in range(0, h_scr.shape[0], SUB_M):
            rs = slice(m0, m0 + SUB_M)
            y = o_ref[rs, :] + partial_sum(rs)
            o_ref[rs, :] = x1_ref[rs, :] + mod_ref[0, 3:4, :] * _rms_norm(y, postw_ref[...])


def _mlp(x1, mod3, prew, postw, w_up_bf, w_down_bf, tm, tf, rows_per_batch):
    m, d = x1.shape
    ff = w_up_bf.shape[1]
    assert ff // tf >= 2, "the first and last hidden chunks take different branches"
    tiles_per_batch = rows_per_batch // tm
    vec = pl.BlockSpec((1, d), lambda i, k: (0, 0))
    return pl.pallas_call(
        _mlp_kernel,
        grid=(m // tm, ff // tf),
        in_specs=[pl.BlockSpec((tm, d), lambda i, k: (i, 0)),
                  pl.BlockSpec((1,) + mod3.shape[1:], lambda i, k: (i // tiles_per_batch, 0, 0)),
                  vec, vec,
                  pl.BlockSpec((d, tf), lambda i, k: (0, k)),
                  pl.BlockSpec((tf, d), lambda i, k: (k, 0))],
        out_specs=pl.BlockSpec((tm, d), lambda i, k: (i, 0)),
        out_shape=jax.ShapeDtypeStruct((m, d), F32),
        scratch_shapes=[pltpu.VMEM((tm, d), BF16)],
        compiler_params=_params(2),
        name="mlp",
    )(x1, mod3, prew, postw, w_up_bf, w_down_bf)


def _tile(n, target):
    t = min(n, target)
    assert n % t == 0, (n, t)
    return t


def kernel(x, c, ctx, c_ctx, w_ada, b_ada, mix_pre_w, mix_post_w, mlp_pre_w, mlp_post_w, w_in, conv_w, conv_b,
           conv_ln_w, conv_ln_b, rec_lb_logits, rec_norm_w, w_out, w_up, w_down):
    bsz, seq, d = x.shape
    assert w_in.shape[0] == 1, "single-layer block"
    gw = d // 2
    assert gw % REC_DK == 0 and seq % REC_CHUNK == 0 and ctx.shape[1] % SUBLANES == 0
    m = bsz * seq

    cc = jnp.concatenate([c, c_ctx[None, :], jnp.zeros((SUBLANES - bsz - 1, d), c.dtype)], axis=0)
    mod_a = _ada(cc, w_ada[0], b_ada, _tile(2 * d, 1024), 2 * d).reshape(SUBLANES, 2, d)

    lbl = rec_lb_logits.astype(F32)

    s0f, s0b = _ctx_states(ctx, mod_a, mix_pre_w, lbl, w_in[0], col0=2 * gw + 2 * gw,
                           heads_per_step=min(4, gw // REC_DK))

    x2 = x.reshape(m, d)
    tm = _tile(seq, 512)
    z2 = _inproj(x2, mod_a, mix_pre_w, lbl, w_in[0], _tile(seq, 1024), _tile(gw, 512), seq)
    z3 = z2.reshape(bsz, seq, 6 * gw)

    tl = _tile(seq, 512)
    uconv, mod_b, (w_out_bf, w_up_bf, w_down_bf) = _conv(
        z3, conv_w[0], conv_b, conv_ln_w, conv_ln_b, tl, _tile(tl, 64), [w_out[0], w_up[0], w_down[0]],
        cc, w_ada[0], b_ada, ada_col0=2 * d)
    mod_b = mod_b.reshape(SUBLANES, 4, d)
    o_f, o_b = _gla(z3, s0f, s0b, _tile(seq, 256), gw)

    x1 = _outproj(uconv.reshape(m, gw), o_f.reshape(m, gw), o_b.reshape(m, gw), z2, x2, mod_b,
                  rec_norm_w, mix_post_w, w_out_bf, _tile(seq, 512), seq)
    out = _mlp(x1, mod_b, mlp_pre_w, mlp_post_w, w_up_bf, w_down_bf, tm, _tile(w_up.shape[2], 1024), seq)
    return out.reshape(bsz, seq, d)
```

```python
import functools

import jax
import jax.numpy as jnp
from jax import lax
from jax.experimental import pallas as pl
from jax.experimental.pallas import tpu as pltpu

NORM_EPS = 1e-6
REC_DK = 128
REC_CHUNK = 64
CONV_K = 31
CONV_HALO = 16
CONV_SLAB = 256
SUBLANES = 8
SUB_M = 256
PROJ_SUB_M = 512
VMEM_LIMIT_BYTES = 56 * 1024 * 1024

F32 = jnp.float32
BF16 = jnp.bfloat16


def _params(n_axes):
    return pltpu.CompilerParams(dimension_semantics=("arbitrary",) * n_axes,
                                vmem_limit_bytes=VMEM_LIMIT_BYTES)


def _sigmoid(x):
    return 1.0 / (1.0 + jnp.exp(-x))


def _rms_norm(xf, w):
    ms = jnp.mean(xf * xf, axis=-1, keepdims=True)
    return xf * lax.rsqrt(ms + NORM_EPS) * w


def _dot(a, b):
    return jnp.dot(a, b, preferred_element_type=F32)


def _dot_nt(a, b):
    return lax.dot_general(a, b, (((1,), (1,)), ((), ())), preferred_element_type=F32)


def _dot_tn(a, b):
    return lax.dot_general(a, b, (((0,), (0,)), ((), ())), preferred_element_type=F32)


def _tri_sum(tri, g):
    if g.dtype == BF16:
        return _dot(tri, g)
    g_hi = g.astype(BF16)
    g_lo = (g - g_hi.astype(F32)).astype(BF16)
    return _dot(tri, g_hi) + _dot(tri, g_lo)


def _lower_bound(lbl, direction):
    m = jnp.max(lbl, axis=0)
    e = jnp.exp(lbl - m[None])
    sm0 = e[0] / jnp.sum(e, axis=0)
    return sm0[direction:direction + 1]


def _log_forget(z, lb):
    return jnp.log2(lb + (1.0 - lb) * _sigmoid(z))


def _ada_kernel(c_ref, w_ref, b_ref, o_ref):
    c = c_ref[...]
    s = (c * _sigmoid(c)).astype(BF16)
    o_ref[...] = _dot(s, w_ref[...].astype(BF16)) + b_ref[...]


def _ada(cc, w_ada, b_ada, tn, n_cols):
    rows, d = cc.shape
    return pl.pallas_call(
        _ada_kernel,
        grid=(n_cols // tn,),
        in_specs=[pl.BlockSpec((rows, d), lambda j: (0, 0)),
                  pl.BlockSpec((d, tn), lambda j: (0, j)),
                  pl.BlockSpec((1, tn), lambda j: (0, j))],
        out_specs=pl.BlockSpec((rows, tn), lambda j: (0, j)),
        out_shape=jax.ShapeDtypeStruct((rows, n_cols), F32),
        compiler_params=_params(1),
        name="ada",
    )(cc, w_ada, b_ada)


def _ctx_kernel(ctx_ref, mod_ref, npw_ref, lbl_ref, wi_ref, wf_ref, wb_ref, sf_ref, sb_ref, h_scr):
    @pl.when(pl.program_id(1) == 0)
    def _():
        h = _rms_norm(ctx_ref[0], npw_ref[...])
        h_scr[...] = (h * (1.0 + mod_ref[0, 1:2, :]) + mod_ref[0, 0:1, :]).astype(BF16)

    hb = h_scr[...]
    lc = hb.shape[0]
    v = _dot(hb, wi_ref[...].astype(BF16)).astype(BF16)
    lbl = lbl_ref[...]
    g_f = _log_forget(_dot(hb, wf_ref[...].astype(BF16)), _lower_bound(lbl, 0))
    g_b = _log_forget(_dot(hb, wb_ref[...].astype(BF16)), _lower_bound(lbl, 1))
    row = lax.broadcasted_iota(jnp.int32, (lc, lc), 0)
    col = lax.broadcasted_iota(jnp.int32, (lc, lc), 1)
    incl = (col <= row).astype(BF16)
    excl = (col < row).astype(BF16)
    b_f = _tri_sum(incl, g_f)
    kd_f = ((1.0 - jnp.exp2(g_f)) * jnp.exp2(b_f[lc - 1:lc, :] - b_f)).astype(BF16)
    c_b = _tri_sum(excl, g_b)
    kd_b = ((1.0 - jnp.exp2(g_b)) * jnp.exp2(c_b)).astype(BF16)
    for h in range(v.shape[1] // REC_DK):
        ls = slice(h * REC_DK, (h + 1) * REC_DK)
        sf_ref[0, h] = _dot_tn(v[:, ls], kd_f[:, ls])
        sb_ref[0, h] = _dot_tn(v[:, ls], kd_b[:, ls])


def _ctx_states(ctx, mod3, npw, lbl, w_in, col0, heads_per_step):
    bsz, lc, d = ctx.shape
    nh = lbl.shape[-1] // REC_DK
    hg = heads_per_step
    wcols = hg * REC_DK
    cb = col0 // wcols
    ng = nh // hg
    wspec = lambda off: pl.BlockSpec((d, wcols), lambda b, g, off=off: (0, cb + off * ng + g))
    st_spec = pl.BlockSpec((1, hg, REC_DK, REC_DK), lambda b, g: (b, g, 0, 0))
    st_shape = jax.ShapeDtypeStruct((bsz, nh, REC_DK, REC_DK), F32)
    return pl.pallas_call(
        _ctx_kernel,
        grid=(bsz, ng),
        in_specs=[pl.BlockSpec((1, lc, d), lambda b, g: (b, 0, 0)),
                  pl.BlockSpec((1,) + mod3.shape[1:], lambda b, g: (bsz, 0, 0)),
                  pl.BlockSpec((1, d), lambda b, g: (0, 0)),
                  pl.BlockSpec(lbl.shape[:2] + (wcols,), lambda b, g: (0, 0, g)),
                  wspec(0), wspec(1), wspec(2)],
        out_specs=[st_spec, st_spec],
        out_shape=[st_shape, st_shape],
        scratch_shapes=[pltpu.VMEM((lc, d), BF16)],
        compiler_params=_params(2),
        name="ctx_states",
    )(ctx, mod3, npw, lbl, w_in, w_in, w_in)


def _project_tiles(h_scr, w_ref, wb_scr, tn, epilogue):
    for m0 in range(0, h_scr.shape[0], PROJ_SUB_M):
        rs = slice(m0, m0 + PROJ_SUB_M)
        for n0 in range(0, w_ref.shape[1], tn):
            cs = slice(n0, n0 + tn)
            if m0 == 0:
                wb_scr[:, cs] = w_ref[:, cs].astype(BF16)
            epilogue(rs, cs, _dot(h_scr[rs, :], wb_scr[:, cs]))


def _inproj_kernel(x_ref, mod_ref, npw_ref, lbl_ref, w_ref, o_ref, h_scr, a_scr, wb_scr, *, tn):
    j = pl.program_id(1)

    @pl.when(j == 0)
    def _():
        h = _rms_norm(x_ref[...], npw_ref[...])
        h_scr[...] = (h * (1.0 + mod_ref[0, 1:2, :]) + mod_ref[0, 0:1, :]).astype(BF16)

        def park(rs, cs, z):
            a_scr[rs, cs] = z
        _project_tiles(h_scr, w_ref, wb_scr, tn, park)

    @pl.when(j == 1)
    def _():
        def glu(rs, cs, z):
            o_ref[rs, cs] = (a_scr[rs, cs] * _sigmoid(z)).astype(o_ref.dtype)
        _project_tiles(h_scr, w_ref, wb_scr, tn, glu)

    @pl.when((j == 2) | (j == 3))
    def _():
        def swish(rs, cs, z):
            o_ref[rs, cs] = (z * _sigmoid(z)).astype(o_ref.dtype)
        _project_tiles(h_scr, w_ref, wb_scr, tn, swish)

    @pl.when(j == 4)
    def _():
        def values(rs, cs, z):
            o_ref[rs, cs] = z.astype(o_ref.dtype)
        _project_tiles(h_scr, w_ref, wb_scr, tn, values)

    @pl.when(j >= 5)
    def _():
        lbl = lbl_ref[...]
        lb = jnp.where(j == 5, _lower_bound(lbl, 0), _lower_bound(lbl, 1))

        def log_gate(rs, cs, z):
            o_ref[rs, cs] = _log_forget(z, lb[:, cs]).astype(o_ref.dtype)
        _project_tiles(h_scr, w_ref, wb_scr, tn, log_gate)


def _inproj(x2, mod3, npw, lbl, w_in, tm, tn, rows_per_batch):
    m, d = x2.shape
    gw = d // 2
    n_groups = w_in.shape[1] // gw
    tiles_per_batch = rows_per_batch // tm
    return pl.pallas_call(
        functools.partial(_inproj_kernel, tn=tn),
        grid=(m // tm, n_groups),
        in_specs=[pl.BlockSpec((tm, d), lambda i, j: (i, 0)),
                  pl.BlockSpec((1,) + mod3.shape[1:], lambda i, j: (i // tiles_per_batch, 0, 0)),
                  pl.BlockSpec((1, d), lambda i, j: (0, 0)),
                  pl.BlockSpec(lbl.shape, lambda i, j: (0, 0, 0)),
                  pl.BlockSpec((d, gw), lambda i, j: (0, j))],
        out_specs=pl.BlockSpec((tm, gw), lambda i, j: (i, jnp.maximum(j - 1, 0))),
        out_shape=jax.ShapeDtypeStruct((m, (n_groups - 1) * gw), BF16),
        scratch_shapes=[pltpu.VMEM((tm, d), BF16), pltpu.VMEM((tm, gw), F32), pltpu.VMEM((d, gw), BF16)],
        compiler_params=_params(2),
        name="inproj",
    )(x2, mod3, npw, lbl, w_in)


def _conv_kernel(u_ref, ul_ref, ur_ref, cw_ref, cb_ref, lnw_ref, lnb_ref, cc_ref, wada_ref, bada_ref, *rest,
                 rb, n_cast):
    cast_in, o_ref, modb_ref = rest[:n_cast], rest[n_cast], rest[n_cast + 1]
    cast_out = rest[n_cast + 2:2 * n_cast + 2]
    buf, y_scr = rest[2 * n_cast + 2:]
    for src, dst in zip(cast_in, cast_out):
        dst[...] = src[...].astype(dst.dtype)
    _ada_kernel(cc_ref, wada_ref, bada_ref, modb_ref)

    t = pl.program_id(1)
    tl, ch = u_ref.shape[1], u_ref.shape[2]
    buf[pl.ds(CONV_HALO, tl), :] = u_ref[0]
    buf[pl.ds(0, CONV_HALO), :] = jnp.where(t > 0, ul_ref[0], jnp.zeros_like(ul_ref[0]))
    buf[pl.ds(CONV_HALO + tl, CONV_HALO), :] = jnp.where(t < pl.num_programs(1) - 1, ur_ref[0],
                                                          jnp.zeros_like(ur_ref[0]))

    shift0 = CONV_HALO - CONV_K // 2
    win = rb + 2 * CONV_HALO
    span = win - SUBLANES
    r_i = lax.broadcasted_iota(jnp.int32, (span, win), 0)
    c_i = lax.broadcasted_iota(jnp.int32, (span, win), 1)
    shifts = [(c_i == r_i + p).astype(BF16) for p in range(SUBLANES)]

    def row_block(i, carry):
        r0 = pl.multiple_of(i * rb, rb)

        for l0 in range(0, ch, CONV_SLAB):
            ls = slice(l0, l0 + CONV_SLAB)
            w = buf[pl.ds(r0, win), ls]
            acc = jnp.zeros((rb, CONV_SLAB), F32)
            for p in range(SUBLANES):
                sp = _dot(shifts[p], w)
                for k in range(CONV_K):
                    if (k + shift0) % SUBLANES == p:
                        a0 = (k + shift0) // SUBLANES * SUBLANES
                        acc = acc + sp[a0:a0 + rb, :] * cw_ref[k:k + 1, ls]
            y_scr[pl.ds(r0, rb), ls] = acc
        return carry

    lax.fori_loop(0, tl // rb, row_block, 0)

    y = y_scr[...] + cb_ref[...]
    mu = jnp.mean(y, axis=-1, keepdims=True)
    yc = y - mu
    var = jnp.mean(yc * yc, axis=-1, keepdims=True)
    yn = yc * lax.rsqrt(var + NORM_EPS) * lnw_ref[...] + lnb_ref[...]
    o_ref[0] = (yn * _sigmoid(yn)).astype(o_ref.dtype)


def _conv(z3, conv_w, conv_b, ln_w, ln_b, tl, rb, cast_weights, cc, w_ada, b_ada, ada_col0):
    bsz, seq, _ = z3.shape
    ch = conv_w.shape[1]
    hb = tl // CONV_HALO
    n_halo = seq // CONV_HALO
    nt = seq // tl
    n_steps = bsz * nt
    vec = pl.BlockSpec((1, ch), lambda b, t: (0, 0))
    for w in cast_weights:
        assert w.shape[0] % (n_steps * 16) == 0, w.shape
    cast_specs = [pl.BlockSpec((w.shape[0] // n_steps, w.shape[1]), lambda b, t: (b * nt + t, 0))
                  for w in cast_weights]
    n_ada = w_ada.shape[1] - ada_col0
    ta = n_ada // n_steps
    assert n_ada % n_steps == 0 and ta % 128 == 0 and ada_col0 % ta == 0
    ada_specs = [pl.BlockSpec(cc.shape, lambda b, t: (0, 0)),
                 pl.BlockSpec((w_ada.shape[0], ta), lambda b, t: (0, ada_col0 // ta + b * nt + t)),
                 pl.BlockSpec((1, ta), lambda b, t: (0, ada_col0 // ta + b * nt + t))]
    outs = pl.pallas_call(
        functools.partial(_conv_kernel, rb=rb, n_cast=len(cast_weights)),
        grid=(bsz, nt),
        in_specs=[pl.BlockSpec((1, tl, ch), lambda b, t: (b, t, 0)),
                  pl.BlockSpec((1, CONV_HALO, ch), lambda b, t: (b, jnp.maximum(t * hb - 1, 0), 0)),
                  pl.BlockSpec((1, CONV_HALO, ch), lambda b, t: (b, jnp.minimum((t + 1) * hb, n_halo - 1), 0)),
                  pl.BlockSpec((CONV_K, ch), lambda b, t: (0, 0)),
                  vec, vec, vec] + ada_specs + cast_specs,
        out_specs=[pl.BlockSpec((1, tl, ch), lambda b, t: (b, t, 0)),
                   pl.BlockSpec((cc.shape[0], ta), lambda b, t: (0, b * nt + t))] + cast_specs,
        out_shape=[jax.ShapeDtypeStruct((bsz, seq, ch), BF16), jax.ShapeDtypeStruct((cc.shape[0], n_ada), F32)]
                  + [jax.ShapeDtypeStruct(w.shape, BF16) for w in cast_weights],
        scratch_shapes=[pltpu.VMEM((tl + 2 * CONV_HALO, ch), BF16), pltpu.VMEM((tl, ch), F32)],
        compiler_params=_params(2),
        name="conv",
    )(z3, z3, z3, conv_w, conv_b, ln_w, ln_b, cc, w_ada, b_ada, *cast_weights)
    return outs[0], outs[1], outs[2:]


class _Scan:
    def __init__(self, q_ref, v_ref, g_ref, o_ref, st, scr, backward):
        self.q_ref, self.v_ref, self.g_ref, self.o_ref, self.st = q_ref, v_ref, g_ref, o_ref, st
        self.kd_s, self.qa_s, self.ka_s, self.qe_s, self.dec_s, self.kv_s, self.sb_s = scr
        rows, width = q_ref.shape[1], q_ref.shape[2]
        self.n_chunks, self.n_heads = rows // REC_CHUNK, width // REC_DK
        r_i = lax.broadcasted_iota(jnp.int32, (rows, rows), 0)
        c_i = lax.broadcasted_iota(jnp.int32, (rows, rows), 1)
        same_chunk = (r_i // REC_CHUNK) == (c_i // REC_CHUNK)
        tri = (same_chunk & ((c_i >= r_i) if backward else (c_i <= r_i))).astype(BF16)
        m_r = lax.broadcasted_iota(jnp.int32, (REC_CHUNK, REC_CHUNK), 0)
        m_c = lax.broadcasted_iota(jnp.int32, (REC_CHUNK, REC_CHUNK), 1)
        self.mask = (m_c >= m_r) if backward else (m_c <= m_r)
        ref_f = REC_CHUNK // 2 - 1
        self.last, self.ref = (0, REC_CHUNK - 1 - ref_f) if backward else (REC_CHUNK - 1, ref_f)
        self.order = list(range(self.n_chunks - 1, -1, -1) if backward else range(self.n_chunks))
        self.b = _tri_sum(tri, g_ref[0])
        self.scores = {}

    def _rows(self, c):
        return slice(c * REC_CHUNK, (c + 1) * REC_CHUNK)

    def _heads(self):
        return [(h, slice(h * REC_DK, (h + 1) * REC_DK)) for h in range(self.n_heads)]

    def prep(self, c):
        rs = self._rows(c)
        bc, qc = self.b[rs], self.q_ref[0, rs, :].astype(F32)
        b_last = bc[self.last:self.last + 1, :]
        b_ref = bc[self.ref:self.ref + 1, :]
        k = 1.0 - jnp.exp2(self.g_ref[0, rs, :].astype(F32))
        self.kd_s[rs, :] = (k * jnp.exp2(b_last - bc)).astype(BF16)
        self.qa_s[rs, :] = (qc * jnp.exp2(bc - b_ref)).astype(BF16)
        self.ka_s[rs, :] = (k * jnp.exp2(b_ref - bc)).astype(BF16)
        self.qe_s[rs, :] = (qc * jnp.exp2(bc)).astype(BF16)
        self.dec_s[c:c + 1, :] = jnp.exp2(b_last)

    def intra(self, c):
        rs = self._rows(c)
        for h, ls in self._heads():
            s = _dot_nt(self.qa_s[rs, ls], self.ka_s[rs, ls])
            self.scores[c, h] = jnp.where(self.mask, s, 0.0).astype(BF16)
            self.kv_s[c, h] = _dot_tn(self.v_ref[0, rs, ls], self.kd_s[rs, ls])

    def recur(self, c):
        for h, ls in self._heads():
            s = self.st[h]
            self.sb_s[c, h] = s.astype(BF16)
            self.st[h] = self.dec_s[c:c + 1, ls] * s + self.kv_s[c, h]

    def out(self, c):
        rs = self._rows(c)
        for h, ls in self._heads():
            self.o_ref[0, rs, ls] = (_dot(self.scores[c, h], self.v_ref[0, rs, ls])
                                     + _dot_nt(self.qe_s[rs, ls], self.sb_s[c, h]))


def _gla_kernel(qf_ref, vf_ref, gf_ref, qb_ref, vb_ref, gb_ref, s0f_ref, s0b_ref,
                of_ref, ob_ref, stf, stb, *scr):
    @pl.when(pl.program_id(1) == 0)
    def _():
        stf[...] = s0f_ref[0]
        stb[...] = s0b_ref[0]

    half = len(scr) // 2
    fwd = _Scan(qf_ref, vf_ref, gf_ref, of_ref, stf, scr[:half], backward=False)
    bwd = _Scan(qb_ref, vb_ref, gb_ref, ob_ref, stb, scr[half:], backward=True)
    for cf, cb in zip(fwd.order, bwd.order):
        for scan, c in ((fwd, cf), (bwd, cb)):
            scan.prep(c)
        for scan, c in ((fwd, cf), (bwd, cb)):
            scan.intra(c)
        for scan, c in ((fwd, cf), (bwd, cb)):
            scan.recur(c)
            scan.out(c)


def _gla(z3, s0f, s0b, rows, width):
    bsz, seq, _ = z3.shape
    nb = seq // rows
    nh = width // REC_DK
    nc = rows // REC_CHUNK
    state = pltpu.VMEM((nh, REC_DK, REC_DK), F32)
    per_direction = ([pltpu.VMEM((rows, width), BF16)] * 4
                     + [pltpu.VMEM((max(nc, SUBLANES), width), F32),
                        pltpu.VMEM((nc, nh, REC_DK, REC_DK), F32),
                        pltpu.VMEM((nc, nh, REC_DK, REC_DK), BF16)])
    fwd = lambda col: pl.BlockSpec((1, rows, width), lambda b, n, col=col: (b, n, col))
    bwd = lambda col: pl.BlockSpec((1, rows, width), lambda b, n, col=col: (b, nb - 1 - n, col))
    st_spec = pl.BlockSpec((1, nh, REC_DK, REC_DK), lambda b, n: (b, 0, 0, 0))
    o_shape = jax.ShapeDtypeStruct((bsz, seq, width), F32)
    return pl.pallas_call(
        _gla_kernel,
        grid=(bsz, nb),
        in_specs=[fwd(1), fwd(3), fwd(4), bwd(1), bwd(3), bwd(5), st_spec, st_spec],
        out_specs=[pl.BlockSpec((1, rows, width), lambda b, n: (b, n, 0)),
                   pl.BlockSpec((1, rows, width), lambda b, n: (b, nb - 1 - n, 0))],
        out_shape=[o_shape, o_shape],
        scratch_shapes=[state, state] + per_direction * 2,
        compiler_params=_params(2),
        name="gla",
    )(z3, z3, z3, z3, z3, z3, s0f, s0b)


def _outproj_kernel(u_ref, of_ref, ob_ref, gs_ref, x_ref, mod_ref, rnw_ref, postw_ref, w_ref, x1_ref):
    gw = u_ref.shape[1]
    for m0 in range(0, u_ref.shape[0], SUB_M):
        rs = slice(m0, m0 + SUB_M)
        o = of_ref[rs, :] + ob_ref[rs, :]
        heads = []
        for h in range(gw // REC_DK):
            oh = o[:, h * REC_DK:(h + 1) * REC_DK]
            heads.append(oh * lax.rsqrt(jnp.mean(oh * oh, axis=-1, keepdims=True) + NORM_EPS))
        on = jnp.concatenate(heads, axis=-1) * rnw_ref[...] * gs_ref[rs, :]
        y = _dot(u_ref[rs, :].astype(BF16), w_ref[0:gw, :]) + _dot(on.astype(BF16), w_ref[gw:2 * gw, :])
        x1_ref[rs, :] = x_ref[rs, :] + mod_ref[0, 0:1, :] * _rms_norm(y, postw_ref[...])


def _outproj(uconv, o_f, o_b, z2, x2, mod3, rnw, postw, w_out_bf, tm, rows_per_batch):
    m, d = x2.shape
    gw = d // 2
    tiles_per_batch = rows_per_batch // tm
    half = pl.BlockSpec((tm, gw), lambda i: (i, 0))
    full = pl.BlockSpec((tm, d), lambda i: (i, 0))
    vec = lambda n: pl.BlockSpec((1, n), lambda i: (0, 0))
    return pl.pallas_call(
        _outproj_kernel,
        grid=(m // tm,),
        in_specs=[half, half, half,
                  pl.BlockSpec((tm, gw), lambda i: (i, 2)),
                  full,
                  pl.BlockSpec((1,) + mod3.shape[1:], lambda i: (i // tiles_per_batch, 0, 0)),
                  vec(gw), vec(d),
                  pl.BlockSpec((d, d), lambda i: (0, 0))],
        out_specs=full,
        out_shape=jax.ShapeDtypeStruct((m, d), F32),
        compiler_params=_params(1),
        name="outproj",
    )(uconv, o_f, o_b, z2, x2, mod3, rnw, postw, w_out_bf)


def _mlp_kernel(x1_ref, mod_ref, prew_ref, postw_ref, wu_ref, wd_ref, o_ref, h_scr):
    k = pl.program_id(1)
    last = pl.num_programs(1) - 1

    def partial_sum(rs):
        hid = jnp.maximum(_dot(h_scr[rs, :], wu_ref[...]), 0.0)
        return _dot((hid * hid).astype(BF16), wd_ref[...])

    @pl.when(k == 0)
    def _():
        for m0 in range(0, h_scr.shape[0], SUB_M):
            rs = slice(m0, m0 + SUB_M)
            h = _rms_norm(x1_ref[rs, :], prew_ref[...])
            h_scr[rs, :] = (h * (1.0 + mod_ref[0, 2:3, :]) + mod_ref[0, 1:2, :]).astype(BF16)
            o_ref[rs, :] = partial_sum(rs)

    @pl.when((k > 0) & (k < last))
    def _():
        o_ref[...] += partial_sum(slice(None))

    @pl.when(k == last)
    def _():
        for m0 in range(0, h_scr.shape[0], SUB_M):
            rs = slice(m0, m0 + SUB_M)
            y = o_ref[rs, :] + partial_sum(rs)
            o_ref[rs, :] = x1_ref[rs, :] + mod_ref[0, 3:4, :] * _rms_norm(y, postw_ref[...])


def _mlp(x1, mod3, prew, postw, w_up_bf, w_down_bf, tm, tf, rows_per_batch):
    m, d = x1.shape
    ff = w_up_bf.shape[1]
    assert ff // tf >= 2, "the first and last hidden chunks take different branches"
    tiles_per_batch = rows_per_batch // tm
    vec = pl.BlockSpec((1, d), lambda i, k: (0, 0))
    return pl.pallas_call(
        _mlp_kernel,
        grid=(m // tm, ff // tf),
        in_specs=[pl.BlockSpec((tm, d), lambda i, k: (i, 0)),
                  pl.BlockSpec((1,) + mod3.shape[1:], lambda i, k: (i // tiles_per_batch, 0, 0)),
                  vec, vec,
                  pl.BlockSpec((d, tf), lambda i, k: (0, k)),
                  pl.BlockSpec((tf, d), lambda i, k: (k, 0))],
        out_specs=pl.BlockSpec((tm, d), lambda i, k: (i, 0)),
        out_shape=jax.ShapeDtypeStruct((m, d), F32),
        scratch_shapes=[pltpu.VMEM((tm, d), BF16)],
        compiler_params=_params(2),
        name="mlp",
    )(x1, mod3, prew, postw, w_up_bf, w_down_bf)


def _tile(n, target):
    t = min(n, target)
    assert n % t == 0, (n, t)
    return t


def kernel(x, c, ctx, c_ctx, w_ada, b_ada, mix_pre_w, mix_post_w, mlp_pre_w, mlp_post_w, w_in, conv_w, conv_b,
           conv_ln_w, conv_ln_b, rec_lb_logits, rec_norm_w, w_out, w_up, w_down):
    bsz, seq, d = x.shape
    assert w_in.shape[0] == 1, "single-layer block"
    gw = d // 2
    assert gw % REC_DK == 0 and seq % REC_CHUNK == 0 and ctx.shape[1] % SUBLANES == 0
    m = bsz * seq

    cc = jnp.concatenate([c, c_ctx[None, :], jnp.zeros((SUBLANES - bsz - 1, d), c.dtype)], axis=0)
    mod_a = _ada(cc, w_ada[0], b_ada, _tile(2 * d, 1024), 2 * d).reshape(SUBLANES, 2, d)

    lbl = rec_lb_logits.astype(F32)

    s0f, s0b = _ctx_states(ctx, mod_a, mix_pre_w, lbl, w_in[0], col0=2 * gw + 2 * gw,
                           heads_per_step=min(4, gw // REC_DK))

    x2 = x.reshape(m, d)
    tm = _tile(seq, 512)
    z2 = _inproj(x2, mod_a, mix_pre_w, lbl, w_in[0], _tile(seq, 1024), _tile(gw, 512), seq)
    z3 = z2.reshape(bsz, seq, 6 * gw)

    tl = _tile(seq, 512)
    uconv, mod_b, (w_out_bf, w_up_bf, w_down_bf) = _conv(
        z3, conv_w[0], conv_b, conv_ln_w, conv_ln_b, tl, _tile(tl, 64), [w_out[0], w_up[0], w_down[0]],
        cc, w_ada[0], b_ada, ada_col0=2 * d)
    mod_b = mod_b.reshape(SUBLANES, 4, d)
    o_f, o_b = _gla(z3, s0f, s0b, _tile(seq, 512), gw)

    x1 = _outproj(uconv.reshape(m, gw), o_f.reshape(m, gw), o_b.reshape(m, gw), z2, x2, mod_b,
                  rec_norm_w, mix_post_w, w_out_bf, _tile(seq, 512), seq)
    out = _mlp(x1, mod_b, mlp_pre_w, mlp_post_w, w_up_bf, w_down_bf, tm, _tile(w_up.shape[2], 1024), seq)
    return out.reshape(bsz, seq, d)
```

```python
import functools

import jax
import jax.numpy as jnp
from jax import lax
from jax.experimental import pallas as pl
from jax.experimental.pallas import tpu as pltpu

NORM_EPS = 1e-6
REC_DK = 128
REC_CHUNK = 64
CONV_K = 31
CONV_HALO = 16
CONV_SLAB = 256
SUBLANES = 8
SUB_M = 256
PROJ_SUB_M = 512
VMEM_LIMIT_BYTES = 56 * 1024 * 1024

F32 = jnp.float32
BF16 = jnp.bfloat16


def _params(n_axes):
    return pltpu.CompilerParams(dimension_semantics=("arbitrary",) * n_axes,
                                vmem_limit_bytes=VMEM_LIMIT_BYTES)


def _sigmoid(x):
    return 1.0 / (1.0 + jnp.exp(-x))


def _rms_norm(xf, w):
    ms = jnp.mean(xf * xf, axis=-1, keepdims=True)
    return xf * lax.rsqrt(ms + NORM_EPS) * w


def _dot(a, b):
    return jnp.dot(a, b, preferred_element_type=F32)


def _dot_nt(a, b):
    return lax.dot_general(a, b, (((1,), (1,)), ((), ())), preferred_element_type=F32)


def _dot_tn(a, b):
    return lax.dot_general(a, b, (((0,), (0,)), ((), ())), preferred_element_type=F32)


def _tri_sum(tri, g):
    if g.dtype == BF16:
        return _dot(tri, g)
    g_hi = g.astype(BF16)
    g_lo = (g - g_hi.astype(F32)).astype(BF16)
    return _dot(tri, g_hi) + _dot(tri, g_lo)


def _lower_bound(lbl, direction):
    m = jnp.max(lbl, axis=0)
    e = jnp.exp(lbl - m[None])
    sm0 = e[0] / jnp.sum(e, axis=0)
    return sm0[direction:direction + 1]


def _log_forget(z, lb):
    return jnp.log2(lb + (1.0 - lb) * _sigmoid(z))


def _ada_kernel(c_ref, w_ref, b_ref, o_ref):
    c = c_ref[...]
    s = (c * _sigmoid(c)).astype(BF16)
    o_ref[...] = _dot(s, w_ref[...].astype(BF16)) + b_ref[...]


def _ada(cc, w_ada, b_ada, tn, n_cols):
    rows, d = cc.shape
    return pl.pallas_call(
        _ada_kernel,
        grid=(n_cols // tn,),
        in_specs=[pl.BlockSpec((rows, d), lambda j: (0, 0)),
                  pl.BlockSpec((d, tn), lambda j: (0, j)),
                  pl.BlockSpec((1, tn), lambda j: (0, j))],
        out_specs=pl.BlockSpec((rows, tn), lambda j: (0, j)),
        out_shape=jax.ShapeDtypeStruct((rows, n_cols), F32),
        compiler_params=_params(1),
        name="ada",
    )(cc, w_ada, b_ada)


def _ctx_kernel(ctx_ref, mod_ref, npw_ref, lbl_ref, wi_ref, wf_ref, wb_ref, sf_ref, sb_ref, h_scr):
    @pl.when(pl.program_id(1) == 0)
    def _():
        h = _rms_norm(ctx_ref[0], npw_ref[...])
        h_scr[...] = (h * (1.0 + mod_ref[0, 1:2, :]) + mod_ref[0, 0:1, :]).astype(BF16)

    hb = h_scr[...]
    lc = hb.shape[0]
    v = _dot(hb, wi_ref[...].astype(BF16)).astype(BF16)
    lbl = lbl_ref[...]
    g_f = _log_forget(_dot(hb, wf_ref[...].astype(BF16)), _lower_bound(lbl, 0))
    g_b = _log_forget(_dot(hb, wb_ref[...].astype(BF16)), _lower_bound(lbl, 1))
    row = lax.broadcasted_iota(jnp.int32, (lc, lc), 0)
    col = lax.broadcasted_iota(jnp.int32, (lc, lc), 1)
    incl = (col <= row).astype(BF16)
    excl = (col < row).astype(BF16)
    b_f = _tri_sum(incl, g_f)
    kd_f = ((1.0 - jnp.exp2(g_f)) * jnp.exp2(b_f[lc - 1:lc, :] - b_f)).astype(BF16)
    c_b = _tri_sum(excl, g_b)
    kd_b = ((1.0 - jnp.exp2(g_b)) * jnp.exp2(c_b)).astype(BF16)
    for h in range(v.shape[1] // REC_DK):
        ls = slice(h * REC_DK, (h + 1) * REC_DK)
        sf_ref[0, h] = _dot_tn(v[:, ls], kd_f[:, ls])
        sb_ref[0, h] = _dot_tn(v[:, ls], kd_b[:, ls])


def _ctx_states(ctx, mod3, npw, lbl, w_in, col0, heads_per_step):
    bsz, lc, d = ctx.shape
    nh = lbl.shape[-1] // REC_DK
    hg = heads_per_step
    wcols = hg * REC_DK
    cb = col0 // wcols
    ng = nh // hg
    wspec = lambda off: pl.BlockSpec((d, wcols), lambda b, g, off=off: (0, cb + off * ng + g))
    st_spec = pl.BlockSpec((1, hg, REC_DK, REC_DK), lambda b, g: (b, g, 0, 0))
    st_shape = jax.ShapeDtypeStruct((bsz, nh, REC_DK, REC_DK), F32)
    return pl.pallas_call(
        _ctx_kernel,
        grid=(bsz, ng),
        in_specs=[pl.BlockSpec((1, lc, d), lambda b, g: (b, 0, 0)),
                  pl.BlockSpec((1,) + mod3.shape[1:], lambda b, g: (bsz, 0, 0)),
                  pl.BlockSpec((1, d), lambda b, g: (0, 0)),
                  pl.BlockSpec(lbl.shape[:2] + (wcols,), lambda b, g: (0, 0, g)),
                  wspec(0), wspec(1), wspec(2)],
        out_specs=[st_spec, st_spec],
        out_shape=[st_shape, st_shape],
        scratch_shapes=[pltpu.VMEM((lc, d), BF16)],
        compiler_params=_params(2),
        name="ctx_states",
    )(ctx, mod3, npw, lbl, w_in, w_in, w_in)


def _project_tiles(h_scr, w_ref, wb_scr, tn, epilogue):
    for m0 in range(0, h_scr.shape[0], PROJ_SUB_M):
        rs = slice(m0, m0 + PROJ_SUB_M)
        for n0 in range(0, w_ref.shape[1], tn):
            cs = slice(n0, n0 + tn)
            if m0 == 0:
                wb_scr[:, cs] = w_ref[:, cs].astype(BF16)
            epilogue(rs, cs, _dot(h_scr[rs, :], wb_scr[:, cs]))


def _inproj_kernel(x_ref, mod_ref, npw_ref, lbl_ref, w_ref, o_ref, h_scr, a_scr, wb_scr, *, tn):
    j = pl.program_id(1)

    @pl.when(j == 0)
    def _():
        h = _rms_norm(x_ref[...], npw_ref[...])
        h_scr[...] = (h * (1.0 + mod_ref[0, 1:2, :]) + mod_ref[0, 0:1, :]).astype(BF16)

        def park(rs, cs, z):
            a_scr[rs, cs] = z
        _project_tiles(h_scr, w_ref, wb_scr, tn, park)

    @pl.when(j == 1)
    def _():
        def glu(rs, cs, z):
            o_ref[rs, cs] = (a_scr[rs, cs] * _sigmoid(z)).astype(o_ref.dtype)
        _project_tiles(h_scr, w_ref, wb_scr, tn, glu)

    @pl.when((j == 2) | (j == 3))
    def _():
        def swish(rs, cs, z):
            o_ref[rs, cs] = (z * _sigmoid(z)).astype(o_ref.dtype)
        _project_tiles(h_scr, w_ref, wb_scr, tn, swish)

    @pl.when(j == 4)
    def _():
        def values(rs, cs, z):
            o_ref[rs, cs] = z.astype(o_ref.dtype)
        _project_tiles(h_scr, w_ref, wb_scr, tn, values)

    @pl.when(j >= 5)
    def _():
        lbl = lbl_ref[...]
        lb = jnp.where(j == 5, _lower_bound(lbl, 0), _lower_bound(lbl, 1))

        def log_gate(rs, cs, z):
            o_ref[rs, cs] = _log_forget(z, lb[:, cs]).astype(o_ref.dtype)
        _project_tiles(h_scr, w_ref, wb_scr, tn, log_gate)


def _inproj(x2, mod3, npw, lbl, w_in, tm, tn, rows_per_batch):
    m, d = x2.shape
    gw = d // 2
    n_groups = w_in.shape[1] // gw
    tiles_per_batch = rows_per_batch // tm
    return pl.pallas_call(
        functools.partial(_inproj_kernel, tn=tn),
        grid=(m // tm, n_groups),
        in_specs=[pl.BlockSpec((tm, d), lambda i, j: (i, 0)),
                  pl.BlockSpec((1,) + mod3.shape[1:], lambda i, j: (i // tiles_per_batch, 0, 0)),
                  pl.BlockSpec((1, d), lambda i, j: (0, 0)),
                  pl.BlockSpec(lbl.shape, lambda i, j: (0, 0, 0)),
                  pl.BlockSpec((d, gw), lambda i, j: (0, j))],
        out_specs=pl.BlockSpec((tm, gw), lambda i, j: (i, jnp.maximum(j - 1, 0))),
        out_shape=jax.ShapeDtypeStruct((m, (n_groups - 1) * gw), BF16),
        scratch_shapes=[pltpu.VMEM((tm, d), BF16), pltpu.VMEM((tm, gw), F32), pltpu.VMEM((d, gw), BF16)],
        compiler_params=_params(2),
        name="inproj",
    )(x2, mod3, npw, lbl, w_in)


def _conv_kernel(u_ref, ul_ref, ur_ref, cw_ref, cb_ref, lnw_ref, lnb_ref, cc_ref, wada_ref, bada_ref, *rest,
                 rb, n_cast):
    cast_in, o_ref, modb_ref = rest[:n_cast], rest[n_cast], rest[n_cast + 1]
    cast_out = rest[n_cast + 2:2 * n_cast + 2]
    buf, y_scr = rest[2 * n_cast + 2:]
    for src, dst in zip(cast_in, cast_out):
        dst[...] = src[...].astype(dst.dtype)
    _ada_kernel(cc_ref, wada_ref, bada_ref, modb_ref)

    t = pl.program_id(1)
    tl, ch = u_ref.shape[1], u_ref.shape[2]
    buf[pl.ds(CONV_HALO, tl), :] = u_ref[0]
    buf[pl.ds(0, CONV_HALO), :] = jnp.where(t > 0, ul_ref[0], jnp.zeros_like(ul_ref[0]))
    buf[pl.ds(CONV_HALO + tl, CONV_HALO), :] = jnp.where(t < pl.num_programs(1) - 1, ur_ref[0],
                                                          jnp.zeros_like(ur_ref[0]))

    shift0 = CONV_HALO - CONV_K // 2
    win = rb + 2 * CONV_HALO
    span = win - SUBLANES
    r_i = lax.broadcasted_iota(jnp.int32, (span, win), 0)
    c_i = lax.broadcasted_iota(jnp.int32, (span, win), 1)
    shifts = [(c_i == r_i + p).astype(BF16) for p in range(SUBLANES)]

    def row_block(i, carry):
        r0 = pl.multiple_of(i * rb, rb)

        for l0 in range(0, ch, CONV_SLAB):
            ls = slice(l0, l0 + CONV_SLAB)
            w = buf[pl.ds(r0, win), ls]
            acc = jnp.zeros((rb, CONV_SLAB), F32)
            for p in range(SUBLANES):
                sp = _dot(shifts[p], w)
                for k in range(CONV_K):
                    if (k + shift0) % SUBLANES == p:
                        a0 = (k + shift0) // SUBLANES * SUBLANES
                        acc = acc + sp[a0:a0 + rb, :] * cw_ref[k:k + 1, ls]
            y_scr[pl.ds(r0, rb), ls] = acc
        return carry

    lax.fori_loop(0, tl // rb, row_block, 0, unroll=True)

    y = y_scr[...] + cb_ref[...]
    mu = jnp.mean(y, axis=-1, keepdims=True)
    yc = y - mu
    var = jnp.mean(yc * yc, axis=-1, keepdims=True)
    yn = yc * lax.rsqrt(var + NORM_EPS) * lnw_ref[...] + lnb_ref[...]
    o_ref[0] = (yn * _sigmoid(yn)).astype(o_ref.dtype)


def _conv(z3, conv_w, conv_b, ln_w, ln_b, tl, rb, cast_weights, cc, w_ada, b_ada, ada_col0):
    bsz, seq, _ = z3.shape
    ch = conv_w.shape[1]
    hb = tl // CONV_HALO
    n_halo = seq // CONV_HALO
    nt = seq // tl
    n_steps = bsz * nt
    vec = pl.BlockSpec((1, ch), lambda b, t: (0, 0))
    for w in cast_weights:
        assert w.shape[0] % (n_steps * 16) == 0, w.shape
    cast_specs = [pl.BlockSpec((w.shape[0] // n_steps, w.shape[1]), lambda b, t: (b * nt + t, 0))
                  for w in cast_weights]
    n_ada = w_ada.shape[1] - ada_col0
    ta = n_ada // n_steps
    assert n_ada % n_steps == 0 and ta % 128 == 0 and ada_col0 % ta == 0
    ada_specs = [pl.BlockSpec(cc.shape, lambda b, t: (0, 0)),
                 pl.BlockSpec((w_ada.shape[0], ta), lambda b, t: (0, ada_col0 // ta + b * nt + t)),
                 pl.BlockSpec((1, ta), lambda b, t: (0, ada_col0 // ta + b * nt + t))]
    outs = pl.pallas_call(
        functools.partial(_conv_kernel, rb=rb, n_cast=len(cast_weights)),
        grid=(bsz, nt),
        in_specs=[pl.BlockSpec((1, tl, ch), lambda b, t: (b, t, 0)),
                  pl.BlockSpec((1, CONV_HALO, ch), lambda b, t: (b, jnp.maximum(t * hb - 1, 0), 0)),
                  pl.BlockSpec((1, CONV_HALO, ch), lambda b, t: (b, jnp.minimum((t + 1) * hb, n_halo - 1), 0)),
                  pl.BlockSpec((CONV_K, ch), lambda b, t: (0, 0)),
                  vec, vec, vec] + ada_specs + cast_specs,
        out_specs=[pl.BlockSpec((1, tl, ch), lambda b, t: (b, t, 0)),
                   pl.BlockSpec((cc.shape[0], ta), lambda b, t: (0, b * nt + t))] + cast_specs,
        out_shape=[jax.ShapeDtypeStruct((bsz, seq, ch), BF16), jax.ShapeDtypeStruct((cc.shape[0], n_ada), F32)]
                  + [jax.ShapeDtypeStruct(w.shape, BF16) for w in cast_weights],
        scratch_shapes=[pltpu.VMEM((tl + 2 * CONV_HALO, ch), BF16), pltpu.VMEM((tl, ch), F32)],
        compiler_params=_params(2),
        name="conv",
    )(z3, z3, z3, conv_w, conv_b, ln_w, ln_b, cc, w_ada, b_ada, *cast_weights)
    return outs[0], outs[1], outs[2:]


class _Scan:
    def __init__(self, q_ref, v_ref, g_ref, o_ref, st, scr, backward):
        self.q_ref, self.v_ref, self.g_ref, self.o_ref, self.st = q_ref, v_ref, g_ref, o_ref, st
        self.kd_s, self.qa_s, self.ka_s, self.qe_s, self.dec_s, self.kv_s, self.sb_s = scr
        rows, width = q_ref.shape[1], q_ref.shape[2]
        self.n_chunks, self.n_heads = rows // REC_CHUNK, width // REC_DK
        r_i = lax.broadcasted_iota(jnp.int32, (rows, rows), 0)
        c_i = lax.broadcasted_iota(jnp.int32, (rows, rows), 1)
        same_chunk = (r_i // REC_CHUNK) == (c_i // REC_CHUNK)
        tri = (same_chunk & ((c_i >= r_i) if backward else (c_i <= r_i))).astype(BF16)
        m_r = lax.broadcasted_iota(jnp.int32, (REC_CHUNK, REC_CHUNK), 0)
        m_c = lax.broadcasted_iota(jnp.int32, (REC_CHUNK, REC_CHUNK), 1)
        self.mask = (m_c >= m_r) if backward else (m_c <= m_r)
        ref_f = REC_CHUNK // 2 - 1
        self.last, self.ref = (0, REC_CHUNK - 1 - ref_f) if backward else (REC_CHUNK - 1, ref_f)
        self.order = list(range(self.n_chunks - 1, -1, -1) if backward else range(self.n_chunks))
        self.b = _tri_sum(tri, g_ref[0])
        self.scores = {}

    def _rows(self, c):
        return slice(c * REC_CHUNK, (c + 1) * REC_CHUNK)

    def _heads(self):
        return [(h, slice(h * REC_DK, (h + 1) * REC_DK)) for h in range(self.n_heads)]

    def prep(self, c):
        rs = self._rows(c)
        bc, qc = self.b[rs], self.q_ref[0, rs, :].astype(F32)
        b_last = bc[self.last:self.last + 1, :]
        b_ref = bc[self.ref:self.ref + 1, :]
        k = 1.0 - jnp.exp2(self.g_ref[0, rs, :].astype(F32))
        self.kd_s[rs, :] = (k * jnp.exp2(b_last - bc)).astype(BF16)
        self.qa_s[rs, :] = (qc * jnp.exp2(bc - b_ref)).astype(BF16)
        self.ka_s[rs, :] = (k * jnp.exp2(b_ref - bc)).astype(BF16)
        self.qe_s[rs, :] = (qc * jnp.exp2(bc)).astype(BF16)
        self.dec_s[c:c + 1, :] = jnp.exp2(b_last)

    def intra(self, c):
        rs = self._rows(c)
        for h, ls in self._heads():
            s = _dot_nt(self.qa_s[rs, ls], self.ka_s[rs, ls])
            self.scores[c, h] = jnp.where(self.mask, s, 0.0).astype(BF16)
            self.kv_s[c, h] = _dot_tn(self.v_ref[0, rs, ls], self.kd_s[rs, ls])

    def recur(self, c):
        for h, ls in self._heads():
            s = self.st[h]
            self.sb_s[c, h] = s.astype(BF16)
            self.st[h] = self.dec_s[c:c + 1, ls] * s + self.kv_s[c, h]

    def out(self, c):
        rs = self._rows(c)
        for h, ls in self._heads():
            self.o_ref[0, rs, ls] = (_dot(self.scores[c, h], self.v_ref[0, rs, ls])
                                     + _dot_nt(self.qe_s[rs, ls], self.sb_s[c, h]))


def _gla_kernel(qf_ref, vf_ref, gf_ref, qb_ref, vb_ref, gb_ref, s0f_ref, s0b_ref,
                of_ref, ob_ref, stf, stb, *scr):
    @pl.when(pl.program_id(1) == 0)
    def _():
        stf[...] = s0f_ref[0]
        stb[...] = s0b_ref[0]

    half = len(scr) // 2
    fwd = _Scan(qf_ref, vf_ref, gf_ref, of_ref, stf, scr[:half], backward=False)
    bwd = _Scan(qb_ref, vb_ref, gb_ref, ob_ref, stb, scr[half:], backward=True)
    for cf, cb in zip(fwd.order, bwd.order):
        for scan, c in ((fwd, cf), (bwd, cb)):
            scan.prep(c)
        for scan, c in ((fwd, cf), (bwd, cb)):
            scan.intra(c)
        for scan, c in ((fwd, cf), (bwd, cb)):
            scan.recur(c)
            scan.out(c)


def _gla(z3, s0f, s0b, rows, width):
    bsz, seq, _ = z3.shape
    nb = seq // rows
    nh = width // REC_DK
    nc = rows // REC_CHUNK
    state = pltpu.VMEM((nh, REC_DK, REC_DK), F32)
    per_direction = ([pltpu.VMEM((rows, width), BF16)] * 4
                     + [pltpu.VMEM((max(nc, SUBLANES), width), F32),
                        pltpu.VMEM((nc, nh, REC_DK, REC_DK), F32),
                        pltpu.VMEM((nc, nh, REC_DK, REC_DK), BF16)])
    fwd = lambda col: pl.BlockSpec((1, rows, width), lambda b, n, col=col: (b, n, col))
    bwd = lambda col: pl.BlockSpec((1, rows, width), lambda b, n, col=col: (b, nb - 1 - n, col))
    st_spec = pl.BlockSpec((1, nh, REC_DK, REC_DK), lambda b, n: (b, 0, 0, 0))
    o_shape = jax.ShapeDtypeStruct((bsz, seq, width), F32)
    return pl.pallas_call(
        _gla_kernel,
        grid=(bsz, nb),
        in_specs=[fwd(1), fwd(3), fwd(4), bwd(1), bwd(3), bwd(5), st_spec, st_spec],
        out_specs=[pl.BlockSpec((1, rows, width), lambda b, n: (b, n, 0)),
                   pl.BlockSpec((1, rows, width), lambda b, n: (b, nb - 1 - n, 0))],
        out_shape=[o_shape, o_shape],
        scratch_shapes=[state, state] + per_direction * 2,
        compiler_params=_params(2),
        name="gla",
    )(z3, z3, z3, z3, z3, z3, s0f, s0b)


def _outproj_kernel(u_ref, of_ref, ob_ref, gs_ref, x_ref, mod_ref, rnw_ref, postw_ref, w_ref, x1_ref):
    gw = u_ref.shape[1]
    o = of_ref[...] + ob_ref[...]
    heads = []
    for h in range(gw // REC_DK):
        oh = o[:, h * REC_DK:(h + 1) * REC_DK]
        heads.append(oh * lax.rsqrt(jnp.mean(oh * oh, axis=-1, keepdims=True) + NORM_EPS))
    on = jnp.concatenate(heads, axis=-1) * rnw_ref[...] * gs_ref[...]
    y = _dot(u_ref[...].astype(BF16), w_ref[0:gw, :]) + _dot(on.astype(BF16), w_ref[gw:2 * gw, :])
    x1_ref[...] = x_ref[...] + mod_ref[0, 0:1, :] * _rms_norm(y, postw_ref[...])


def _outproj(uconv, o_f, o_b, z2, x2, mod3, rnw, postw, w_out_bf, tm, rows_per_batch):
    m, d = x2.shape
    gw = d // 2
    tiles_per_batch = rows_per_batch // tm
    half = pl.BlockSpec((tm, gw), lambda i: (i, 0))
    full = pl.BlockSpec((tm, d), lambda i: (i, 0))
    vec = lambda n: pl.BlockSpec((1, n), lambda i: (0, 0))
    return pl.pallas_call(
        _outproj_kernel,
        grid=(m // tm,),
        in_specs=[half, half, half,
                  pl.BlockSpec((tm, gw), lambda i: (i, 2)),
                  full,
                  pl.BlockSpec((1,) + mod3.shape[1:], lambda i: (i // tiles_per_batch, 0, 0)),
                  vec(gw), vec(d),
                  pl.BlockSpec((d, d), lambda i: (0, 0))],
        out_specs=full,
        out_shape=jax.ShapeDtypeStruct((m, d), F32),
        compiler_params=_params(1),
        name="outproj",
    )(uconv, o_f, o_b, z2, x2, mod3, rnw, postw, w_out_bf)


def _mlp_kernel(x1_ref, mod_ref, prew_ref, postw_ref, wu_ref, wd_ref, o_ref, h_scr):
    k = pl.program_id(1)
    last = pl.num_programs(1) - 1

    def partial_sum(rs):
        hid = jnp.maximum(_dot(h_scr[rs, :], wu_ref[...]), 0.0)
        return _dot((hid * hid).astype(BF16), wd_ref[...])

    @pl.when(k == 0)
    def _():
        for m0 in range(0, h_scr.shape[0], SUB_M):
            rs = slice(m0, m0 + SUB_M)
            h = _rms_norm(x1_ref[rs, :], prew_ref[...])
            h_scr[rs, :] = (h * (1.0 + mod_ref[0, 2:3, :]) + mod_ref[0, 1:2, :]).astype(BF16)
            o_ref[rs, :] = partial_sum(rs)

    @pl.when((k > 0) & (k < last))
    def _():
        o_ref[...] += partial_sum(slice(None))

    @pl.when(k == last)
    def _():
        for m0 in range(0, h_scr.shape[0], SUB_M):
            rs = slice(m0, m0 + SUB_M)
            y = o_ref[rs, :] + partial_sum(rs)
            o_ref[rs, :] = x1_ref[rs, :] + mod_ref[0, 3:4, :] * _rms_norm(y, postw_ref[...])


def _mlp(x1, mod3, prew, postw, w_up_bf, w_down_bf, tm, tf, rows_per_batch):
    m, d = x1.shape
    ff = w_up_bf.shape[1]
    assert ff // tf >= 2, "the first and last hidden chunks take different branches"
    tiles_per_batch = rows_per_batch // tm
    vec = pl.BlockSpec((1, d), lambda i, k: (0, 0))
    return pl.pallas_call(
        _mlp_kernel,
        grid=(m // tm, ff // tf),
        in_specs=[pl.BlockSpec((tm, d), lambda i, k: (i, 0)),
                  pl.BlockSpec((1,) + mod3.shape[1:], lambda i, k: (i // tiles_per_batch, 0, 0)),
                  vec, vec,
                  pl.BlockSpec((d, tf), lambda i, k: (0, k)),
                  pl.BlockSpec((tf, d), lambda i, k: (k, 0))],
        out_specs=pl.BlockSpec((tm, d), lambda i, k: (i, 0)),
        out_shape=jax.ShapeDtypeStruct((m, d), F32),
        scratch_shapes=[pltpu.VMEM((tm, d), BF16)],
        compiler_params=_params(2),
        name="mlp",
    )(x1, mod3, prew, postw, w_up_bf, w_down_bf)


def _tile(n, target):
    t = min(n, target)
    assert n % t == 0, (n, t)
    return t


def kernel(x, c, ctx, c_ctx, w_ada, b_ada, mix_pre_w, mix_post_w, mlp_pre_w, mlp_post_w, w_in, conv_w, conv_b,
           conv_ln_w, conv_ln_b, rec_lb_logits, rec_norm_w, w_out, w_up, w_down):
    bsz, seq, d = x.shape
    assert w_in.shape[0] == 1, "single-layer block"
    gw = d // 2
    assert gw % REC_DK == 0 and seq % REC_CHUNK == 0 and ctx.shape[1] % SUBLANES == 0
    m = bsz * seq

    cc = jnp.concatenate([c, c_ctx[None, :], jnp.zeros((SUBLANES - bsz - 1, d), c.dtype)], axis=0)
    mod_a = _ada(cc, w_ada[0], b_ada, _tile(2 * d, 1024), 2 * d).reshape(SUBLANES, 2, d)

    lbl = rec_lb_logits.astype(F32)

    s0f, s0b = _ctx_states(ctx, mod_a, mix_pre_w, lbl, w_in[0], col0=2 * gw + 2 * gw,
                           heads_per_step=min(4, gw // REC_DK))

    x2 = x.reshape(m, d)
    tm = _tile(seq, 512)
    z2 = _inproj(x2, mod_a, mix_pre_w, lbl, w_in[0], _tile(seq, 1024), _tile(gw, 512), seq)
    z3 = z2.reshape(bsz, seq, 6 * gw)

    tl = _tile(seq, 512)
    uconv, mod_b, (w_out_bf, w_up_bf, w_down_bf) = _conv(
        z3, conv_w[0], conv_b, conv_ln_w, conv_ln_b, tl, _tile(tl, 64), [w_out[0], w_up[0], w_down[0]],
        cc, w_ada[0], b_ada, ada_col0=2 * d)
    mod_b = mod_b.reshape(SUBLANES, 4, d)
    o_f, o_b = _gla(z3, s0f, s0b, _tile(seq, 512), gw)

    x1 = _outproj(uconv.reshape(m, gw), o_f.reshape(m, gw), o_b.reshape(m, gw), z2, x2, mod_b,
                  rec_norm_w, mix_post_w, w_out_bf, _tile(seq, 512), seq)
    out = _mlp(x1, mod_b, mlp_pre_w, mlp_post_w, w_up_bf, w_down_bf, tm, _tile(w_up.shape[2], 1024), seq)
    return out.reshape(bsz, seq, d)
```

```python
import functools

import jax
import jax.numpy as jnp
from jax import lax
from jax.experimental import pallas as pl
from jax.experimental.pallas import tpu as pltpu

NORM_EPS = 1e-6
REC_DK = 128
REC_CHUNK = 64
CONV_K = 31
CONV_HALO = 16
CONV_SLAB = 256
SUBLANES = 8
SUB_M = 256
PROJ_SUB_M = 512
MLP_SUB_M = 512
V7X_VMEM_BYTES = 64 * 1024 * 1024
VMEM_LIMIT_BYTES = 56 * 1024 * 1024
VMEM_COMPILER_ALLOWANCE = 3 * 1024 * 1024

F32 = jnp.float32
BF16 = jnp.bfloat16


def _params(n_axes, vmem_limit_bytes=VMEM_LIMIT_BYTES):
    assert vmem_limit_bytes < V7X_VMEM_BYTES
    return pltpu.CompilerParams(dimension_semantics=("arbitrary",) * n_axes,
                                vmem_limit_bytes=vmem_limit_bytes)


def _sigmoid(x):
    return 1.0 / (1.0 + jnp.exp(-x))


def _rms_norm(xf, w):
    ms = jnp.mean(xf * xf, axis=-1, keepdims=True)
    return xf * lax.rsqrt(ms + NORM_EPS) * w


def _dot(a, b):
    return jnp.dot(a, b, preferred_element_type=F32)


def _dot_nt(a, b):
    return lax.dot_general(a, b, (((1,), (1,)), ((), ())), preferred_element_type=F32)


def _dot_tn(a, b):
    return lax.dot_general(a, b, (((0,), (0,)), ((), ())), preferred_element_type=F32)


def _tri_sum(tri, g):
    if g.dtype == BF16:
        return _dot(tri, g)
    g_hi = g.astype(BF16)
    g_lo = (g - g_hi.astype(F32)).astype(BF16)
    return _dot(tri, g_hi) + _dot(tri, g_lo)


def _lower_bound(lbl, direction):
    m = jnp.max(lbl, axis=0)
    e = jnp.exp(lbl - m[None])
    sm0 = e[0] / jnp.sum(e, axis=0)
    return sm0[direction:direction + 1]


def _log_forget(z, lb):
    return jnp.log2(lb + (1.0 - lb) * _sigmoid(z))


def _ada_kernel(c_ref, w_ref, b_ref, o_ref):
    c = c_ref[...]
    s = (c * _sigmoid(c)).astype(BF16)
    o_ref[...] = _dot(s, w_ref[...].astype(BF16)) + b_ref[...]


def _ada(cc, w_ada, b_ada, tn, n_cols):
    rows, d = cc.shape
    return pl.pallas_call(
        _ada_kernel,
        grid=(n_cols // tn,),
        in_specs=[pl.BlockSpec((rows, d), lambda j: (0, 0)),
                  pl.BlockSpec((d, tn), lambda j: (0, j)),
                  pl.BlockSpec((1, tn), lambda j: (0, j))],
        out_specs=pl.BlockSpec((rows, tn), lambda j: (0, j)),
        out_shape=jax.ShapeDtypeStruct((rows, n_cols), F32),
        compiler_params=_params(1),
        name="ada",
    )(cc, w_ada, b_ada)


def _ctx_kernel(ctx_ref, mod_ref, npw_ref, lbl_ref, wi_ref, wf_ref, wb_ref, sf_ref, sb_ref, h_scr):
    @pl.when(pl.program_id(1) == 0)
    def _():
        h = _rms_norm(ctx_ref[0], npw_ref[...])
        h_scr[...] = (h * (1.0 + mod_ref[0, 1:2, :]) + mod_ref[0, 0:1, :]).astype(BF16)

    hb = h_scr[...]
    lc = hb.shape[0]
    v = _dot(hb, wi_ref[...].astype(BF16)).astype(BF16)
    lbl = lbl_ref[...]
    g_f = _log_forget(_dot(hb, wf_ref[...].astype(BF16)), _lower_bound(lbl, 0))
    g_b = _log_forget(_dot(hb, wb_ref[...].astype(BF16)), _lower_bound(lbl, 1))
    row = lax.broadcasted_iota(jnp.int32, (lc, lc), 0)
    col = lax.broadcasted_iota(jnp.int32, (lc, lc), 1)
    incl = (col <= row).astype(BF16)
    excl = (col < row).astype(BF16)
    b_f = _tri_sum(incl, g_f)
    kd_f = ((1.0 - jnp.exp2(g_f)) * jnp.exp2(b_f[lc - 1:lc, :] - b_f)).astype(BF16)
    c_b = _tri_sum(excl, g_b)
    kd_b = ((1.0 - jnp.exp2(g_b)) * jnp.exp2(c_b)).astype(BF16)
    for h in range(v.shape[1] // REC_DK):
        ls = slice(h * REC_DK, (h + 1) * REC_DK)
        sf_ref[0, h] = _dot_tn(v[:, ls], kd_f[:, ls])
        sb_ref[0, h] = _dot_tn(v[:, ls], kd_b[:, ls])


def _ctx_states(ctx, mod3, npw, lbl, w_in, col0, heads_per_step):
    bsz, lc, d = ctx.shape
    nh = lbl.shape[-1] // REC_DK
    hg = heads_per_step
    wcols = hg * REC_DK
    cb = col0 // wcols
    ng = nh // hg
    wspec = lambda off: pl.BlockSpec((d, wcols), lambda b, g, off=off: (0, cb + off * ng + g))
    st_spec = pl.BlockSpec((1, hg, REC_DK, REC_DK), lambda b, g: (b, g, 0, 0))
    st_shape = jax.ShapeDtypeStruct((bsz, nh, REC_DK, REC_DK), F32)
    return pl.pallas_call(
        _ctx_kernel,
        grid=(bsz, ng),
        in_specs=[pl.BlockSpec((1, lc, d), lambda b, g: (b, 0, 0)),
                  pl.BlockSpec((1,) + mod3.shape[1:], lambda b, g: (bsz, 0, 0)),
                  pl.BlockSpec((1, d), lambda b, g: (0, 0)),
                  pl.BlockSpec(lbl.shape[:2] + (wcols,), lambda b, g: (0, 0, g)),
                  wspec(0), wspec(1), wspec(2)],
        out_specs=[st_spec, st_spec],
        out_shape=[st_shape, st_shape],
        scratch_shapes=[pltpu.VMEM((lc, d), BF16)],
        compiler_params=_params(2),
        name="ctx_states",
    )(ctx, mod3, npw, lbl, w_in, w_in, w_in)


def _project_tiles(h_scr, w_ref, wb_scr, tn, epilogue):
    for m0 in range(0, h_scr.shape[0], PROJ_SUB_M):
        rs = slice(m0, m0 + PROJ_SUB_M)
        for n0 in range(0, w_ref.shape[1], tn):
            cs = slice(n0, n0 + tn)
            if m0 == 0:
                wb_scr[:, cs] = w_ref[:, cs].astype(BF16)
            epilogue(rs, cs, _dot(h_scr[rs, :], wb_scr[:, cs]))


def _inproj_kernel(x_ref, mod_ref, npw_ref, lbl_ref, w_ref, o_ref, h_scr, a_scr, wb_scr, *, tn):
    j = pl.program_id(1)

    @pl.when(j == 0)
    def _():
        h = _rms_norm(x_ref[...], npw_ref[...])
        h_scr[...] = (h * (1.0 + mod_ref[0, 1:2, :]) + mod_ref[0, 0:1, :]).astype(BF16)

        def park(rs, cs, z):
            a_scr[rs, cs] = z
        _project_tiles(h_scr, w_ref, wb_scr, tn, park)

    @pl.when(j == 1)
    def _():
        def glu(rs, cs, z):
            o_ref[rs, cs] = (a_scr[rs, cs] * _sigmoid(z)).astype(o_ref.dtype)
        _project_tiles(h_scr, w_ref, wb_scr, tn, glu)

    @pl.when((j == 2) | (j == 3))
    def _():
        def swish(rs, cs, z):
            o_ref[rs, cs] = (z * _sigmoid(z)).astype(o_ref.dtype)
        _project_tiles(h_scr, w_ref, wb_scr, tn, swish)

    @pl.when(j == 4)
    def _():
        def values(rs, cs, z):
            o_ref[rs, cs] = z.astype(o_ref.dtype)
        _project_tiles(h_scr, w_ref, wb_scr, tn, values)

    @pl.when(j >= 5)
    def _():
        lbl = lbl_ref[...]
        lb = jnp.where(j == 5, _lower_bound(lbl, 0), _lower_bound(lbl, 1))

        def log_gate(rs, cs, z):
            o_ref[rs, cs] = _log_forget(z, lb[:, cs]).astype(o_ref.dtype)
        _project_tiles(h_scr, w_ref, wb_scr, tn, log_gate)


def _inproj(x2, mod3, npw, lbl, w_in, tm, tn, rows_per_batch):
    m, d = x2.shape
    gw = d // 2
    n_groups = w_in.shape[1] // gw
    tiles_per_batch = rows_per_batch // tm
    return pl.pallas_call(
        functools.partial(_inproj_kernel, tn=tn),
        grid=(m // tm, n_groups),
        in_specs=[pl.BlockSpec((tm, d), lambda i, j: (i, 0)),
                  pl.BlockSpec((1,) + mod3.shape[1:], lambda i, j: (i // tiles_per_batch, 0, 0)),
                  pl.BlockSpec((1, d), lambda i, j: (0, 0)),
                  pl.BlockSpec(lbl.shape, lambda i, j: (0, 0, 0)),
                  pl.BlockSpec((d, gw), lambda i, j: (0, j))],
        out_specs=pl.BlockSpec((tm, gw), lambda i, j: (i, jnp.maximum(j - 1, 0))),
        out_shape=jax.ShapeDtypeStruct((m, (n_groups - 1) * gw), BF16),
        scratch_shapes=[pltpu.VMEM((tm, d), BF16), pltpu.VMEM((tm, gw), F32), pltpu.VMEM((d, gw), BF16)],
        compiler_params=_params(2),
        name="inproj",
    )(x2, mod3, npw, lbl, w_in)


def _conv_kernel(u_ref, ul_ref, ur_ref, cw_ref, cb_ref, lnw_ref, lnb_ref, cc_ref, wada_ref, bada_ref, *rest,
                 rb, n_cast):
    cast_in, o_ref, modb_ref = rest[:n_cast], rest[n_cast], rest[n_cast + 1]
    cast_out = rest[n_cast + 2:2 * n_cast + 2]
    buf, y_scr = rest[2 * n_cast + 2:]
    for src, dst in zip(cast_in, cast_out):
        dst[...] = src[...].astype(dst.dtype)
    _ada_kernel(cc_ref, wada_ref, bada_ref, modb_ref)

    t = pl.program_id(1)
    tl, ch = u_ref.shape[1], u_ref.shape[2]
    buf[pl.ds(CONV_HALO, tl), :] = u_ref[0]
    buf[pl.ds(0, CONV_HALO), :] = jnp.where(t > 0, ul_ref[0], jnp.zeros_like(ul_ref[0]))
    buf[pl.ds(CONV_HALO + tl, CONV_HALO), :] = jnp.where(t < pl.num_programs(1) - 1, ur_ref[0],
                                                          jnp.zeros_like(ur_ref[0]))

    shift0 = CONV_HALO - CONV_K // 2
    win = rb + 2 * CONV_HALO
    span = win - SUBLANES
    r_i = lax.broadcasted_iota(jnp.int32, (span, win), 0)
    c_i = lax.broadcasted_iota(jnp.int32, (span, win), 1)
    shifts = [(c_i == r_i + p).astype(BF16) for p in range(SUBLANES)]

    def row_block(i, carry):
        r0 = pl.multiple_of(i * rb, rb)

        for l0 in range(0, ch, CONV_SLAB):
            ls = slice(l0, l0 + CONV_SLAB)
            w = buf[pl.ds(r0, win), ls]
            acc = jnp.zeros((rb, CONV_SLAB), F32)
            for p in range(SUBLANES):
                sp = _dot(shifts[p], w)
                for k in range(CONV_K):
                    if (k + shift0) % SUBLANES == p:
                        a0 = (k + shift0) // SUBLANES * SUBLANES
                        acc = acc + sp[a0:a0 + rb, :] * cw_ref[k:k + 1, ls]
            y_scr[pl.ds(r0, rb), ls] = acc
        return carry

    lax.fori_loop(0, tl // rb, row_block, 0, unroll=True)

    y = y_scr[...] + cb_ref[...]
    mu = jnp.mean(y, axis=-1, keepdims=True)
    yc = y - mu
    var = jnp.mean(yc * yc, axis=-1, keepdims=True)
    yn = yc * lax.rsqrt(var + NORM_EPS) * lnw_ref[...] + lnb_ref[...]
    o_ref[0] = (yn * _sigmoid(yn)).astype(o_ref.dtype)


def _conv(z3, conv_w, conv_b, ln_w, ln_b, tl, rb, cast_weights, cc, w_ada, b_ada, ada_col0):
    bsz, seq, _ = z3.shape
    ch = conv_w.shape[1]
    hb = tl // CONV_HALO
    n_halo = seq // CONV_HALO
    nt = seq // tl
    n_steps = bsz * nt
    vec = pl.BlockSpec((1, ch), lambda b, t: (0, 0))
    for w in cast_weights:
        assert w.shape[0] % (n_steps * 16) == 0, w.shape
    cast_specs = [pl.BlockSpec((w.shape[0] // n_steps, w.shape[1]), lambda b, t: (b * nt + t, 0))
                  for w in cast_weights]
    n_ada = w_ada.shape[1] - ada_col0
    ta = n_ada // n_steps
    assert n_ada % n_steps == 0 and ta % 128 == 0 and ada_col0 % ta == 0
    ada_specs = [pl.BlockSpec(cc.shape, lambda b, t: (0, 0)),
                 pl.BlockSpec((w_ada.shape[0], ta), lambda b, t: (0, ada_col0 // ta + b * nt + t)),
                 pl.BlockSpec((1, ta), lambda b, t: (0, ada_col0 // ta + b * nt + t))]
    outs = pl.pallas_call(
        functools.partial(_conv_kernel, rb=rb, n_cast=len(cast_weights)),
        grid=(bsz, nt),
        in_specs=[pl.BlockSpec((1, tl, ch), lambda b, t: (b, t, 0)),
                  pl.BlockSpec((1, CONV_HALO, ch), lambda b, t: (b, jnp.maximum(t * hb - 1, 0), 0)),
                  pl.BlockSpec((1, CONV_HALO, ch), lambda b, t: (b, jnp.minimum((t + 1) * hb, n_halo - 1), 0)),
                  pl.BlockSpec((CONV_K, ch), lambda b, t: (0, 0)),
                  vec, vec, vec] + ada_specs + cast_specs,
        out_specs=[pl.BlockSpec((1, tl, ch), lambda b, t: (b, t, 0)),
                   pl.BlockSpec((cc.shape[0], ta), lambda b, t: (0, b * nt + t))] + cast_specs,
        out_shape=[jax.ShapeDtypeStruct((bsz, seq, ch), BF16), jax.ShapeDtypeStruct((cc.shape[0], n_ada), F32)]
                  + [jax.ShapeDtypeStruct(w.shape, BF16) for w in cast_weights],
        scratch_shapes=[pltpu.VMEM((tl + 2 * CONV_HALO, ch), BF16), pltpu.VMEM((tl, ch), F32)],
        compiler_params=_params(2),
        name="conv",
    )(z3, z3, z3, conv_w, conv_b, ln_w, ln_b, cc, w_ada, b_ada, *cast_weights)
    return outs[0], outs[1], outs[2:]


class _Scan:
    def __init__(self, q_ref, v_ref, g_ref, o_ref, st, scr, backward):
        self.q_ref, self.v_ref, self.g_ref, self.o_ref, self.st = q_ref, v_ref, g_ref, o_ref, st
        self.kd_s, self.qa_s, self.ka_s, self.qe_s, self.dec_s, self.kv_s, self.sb_s = scr
        rows, width = q_ref.shape[1], q_ref.shape[2]
        self.n_chunks, self.n_heads = rows // REC_CHUNK, width // REC_DK
        r_i = lax.broadcasted_iota(jnp.int32, (rows, rows), 0)
        c_i = lax.broadcasted_iota(jnp.int32, (rows, rows), 1)
        same_chunk = (r_i // REC_CHUNK) == (c_i // REC_CHUNK)
        tri = (same_chunk & ((c_i >= r_i) if backward else (c_i <= r_i))).astype(BF16)
        m_r = lax.broadcasted_iota(jnp.int32, (REC_CHUNK, REC_CHUNK), 0)
        m_c = lax.broadcasted_iota(jnp.int32, (REC_CHUNK, REC_CHUNK), 1)
        self.mask = (m_c >= m_r) if backward else (m_c <= m_r)
        ref_f = REC_CHUNK // 2 - 1
        self.last, self.ref = (0, REC_CHUNK - 1 - ref_f) if backward else (REC_CHUNK - 1, ref_f)
        self.order = list(range(self.n_chunks - 1, -1, -1) if backward else range(self.n_chunks))
        self.b = _tri_sum(tri, g_ref[0])
        self.scores = {}

    def _rows(self, c):
        return slice(c * REC_CHUNK, (c + 1) * REC_CHUNK)

    def _heads(self):
        return [(h, slice(h * REC_DK, (h + 1) * REC_DK)) for h in range(self.n_heads)]

    def prep(self, c):
        rs = self._rows(c)
        bc, qc = self.b[rs], self.q_ref[0, rs, :].astype(F32)
        b_last = bc[self.last:self.last + 1, :]
        b_ref = bc[self.ref:self.ref + 1, :]
        k = 1.0 - jnp.exp2(self.g_ref[0, rs, :].astype(F32))
        self.kd_s[rs, :] = (k * jnp.exp2(b_last - bc)).astype(BF16)
        self.qa_s[rs, :] = (qc * jnp.exp2(bc - b_ref)).astype(BF16)
        self.ka_s[rs, :] = (k * jnp.exp2(b_ref - bc)).astype(BF16)
        self.qe_s[rs, :] = (qc * jnp.exp2(bc)).astype(BF16)
        self.dec_s[c:c + 1, :] = jnp.exp2(b_last)

    def intra(self, c):
        rs = self._rows(c)
        for h, ls in self._heads():
            s = _dot_nt(self.qa_s[rs, ls], self.ka_s[rs, ls])
            self.scores[c, h] = jnp.where(self.mask, s, 0.0).astype(BF16)
            self.kv_s[c, h] = _dot_tn(self.v_ref[0, rs, ls], self.kd_s[rs, ls])

    def recur(self, c):
        for h, ls in self._heads():
            s = self.st[h]
            self.sb_s[c, h] = s.astype(BF16)
            self.st[h] = self.dec_s[c:c + 1, ls] * s + self.kv_s[c, h]

    def out(self, c):
        rs = self._rows(c)
        for h, ls in self._heads():
            self.o_ref[0, rs, ls] = (_dot(self.scores[c, h], self.v_ref[0, rs, ls])
                                     + _dot_nt(self.qe_s[rs, ls], self.sb_s[c, h]))


def _gla_kernel(qf_ref, vf_ref, gf_ref, qb_ref, vb_ref, gb_ref, s0f_ref, s0b_ref,
                of_ref, ob_ref, stf, stb, *scr):
    @pl.when(pl.program_id(1) == 0)
    def _():
        stf[...] = s0f_ref[0]
        stb[...] = s0b_ref[0]

    half = len(scr) // 2
    fwd = _Scan(qf_ref, vf_ref, gf_ref, of_ref, stf, scr[:half], backward=False)
    bwd = _Scan(qb_ref, vb_ref, gb_ref, ob_ref, stb, scr[half:], backward=True)
    for cf, cb in zip(fwd.order, bwd.order):
        for scan, c in ((fwd, cf), (bwd, cb)):
            scan.prep(c)
        for scan, c in ((fwd, cf), (bwd, cb)):
            scan.intra(c)
        for scan, c in ((fwd, cf), (bwd, cb)):
            scan.recur(c)
            scan.out(c)


def _gla(z3, s0f, s0b, rows, width):
    bsz, seq, _ = z3.shape
    nb = seq // rows
    nh = width // REC_DK
    nc = rows // REC_CHUNK
    state = pltpu.VMEM((nh, REC_DK, REC_DK), F32)
    per_direction = ([pltpu.VMEM((rows, width), BF16)] * 4
                     + [pltpu.VMEM((max(nc, SUBLANES), width), F32),
                        pltpu.VMEM((nc, nh, REC_DK, REC_DK), F32),
                        pltpu.VMEM((nc, nh, REC_DK, REC_DK), BF16)])
    fwd = lambda col: pl.BlockSpec((1, rows, width), lambda b, n, col=col: (b, n, col))
    bwd = lambda col: pl.BlockSpec((1, rows, width), lambda b, n, col=col: (b, nb - 1 - n, col))
    st_spec = pl.BlockSpec((1, nh, REC_DK, REC_DK), lambda b, n: (b, 0, 0, 0))
    o_shape = jax.ShapeDtypeStruct((bsz, seq, width), F32)
    return pl.pallas_call(
        _gla_kernel,
        grid=(bsz, nb),
        in_specs=[fwd(1), fwd(3), fwd(4), bwd(1), bwd(3), bwd(5), st_spec, st_spec],
        out_specs=[pl.BlockSpec((1, rows, width), lambda b, n: (b, n, 0)),
                   pl.BlockSpec((1, rows, width), lambda b, n: (b, nb - 1 - n, 0))],
        out_shape=[o_shape, o_shape],
        scratch_shapes=[state, state] + per_direction * 2,
        compiler_params=_params(2),
        name="gla",
    )(z3, z3, z3, z3, z3, z3, s0f, s0b)


def _outproj_kernel(u_ref, of_ref, ob_ref, gs_ref, x_ref, mod_ref, rnw_ref, postw_ref, w_ref, x1_ref):
    gw = u_ref.shape[1]
    o = of_ref[...] + ob_ref[...]
    heads = []
    for h in range(gw // REC_DK):
        oh = o[:, h * REC_DK:(h + 1) * REC_DK]
        heads.append(oh * lax.rsqrt(jnp.mean(oh * oh, axis=-1, keepdims=True) + NORM_EPS))
    on = jnp.concatenate(heads, axis=-1) * rnw_ref[...] * gs_ref[...]
    y = _dot(u_ref[...].astype(BF16), w_ref[0:gw, :]) + _dot(on.astype(BF16), w_ref[gw:2 * gw, :])
    x1_ref[...] = x_ref[...] + mod_ref[0, 0:1, :] * _rms_norm(y, postw_ref[...])


def _outproj(uconv, o_f, o_b, z2, x2, mod3, rnw, postw, w_out_bf, tm, rows_per_batch):
    m, d = x2.shape
    gw = d // 2
    tiles_per_batch = rows_per_batch // tm
    half = pl.BlockSpec((tm, gw), lambda i: (i, 0))
    full = pl.BlockSpec((tm, d), lambda i: (i, 0))
    vec = lambda n: pl.BlockSpec((1, n), lambda i: (0, 0))
    return pl.pallas_call(
        _outproj_kernel,
        grid=(m // tm,),
        in_specs=[half, half, half,
                  pl.BlockSpec((tm, gw), lambda i: (i, 2)),
                  full,
                  pl.BlockSpec((1,) + mod3.shape[1:], lambda i: (i // tiles_per_batch, 0, 0)),
                  vec(gw), vec(d),
                  pl.BlockSpec((d, d), lambda i: (0, 0))],
        out_specs=full,
        out_shape=jax.ShapeDtypeStruct((m, d), F32),
        compiler_params=_params(1),
        name="outproj",
    )(uconv, o_f, o_b, z2, x2, mod3, rnw, postw, w_out_bf)


def _mlp_kernel(x1_ref, mod_ref, prew_ref, postw_ref, wu_ref, wd_ref, o_ref, h_scr):
    k = pl.program_id(1)
    last = pl.num_programs(1) - 1

    def partial_sum(rs):
        hid = jnp.maximum(_dot(h_scr[rs, :], wu_ref[...]), 0.0)
        return _dot((hid * hid).astype(BF16), wd_ref[...])

    @pl.when(k == 0)
    def _():
        for m0 in range(0, h_scr.shape[0], MLP_SUB_M):
            rs = slice(m0, m0 + MLP_SUB_M)
            h = _rms_norm(x1_ref[rs, :], prew_ref[...])
            h_scr[rs, :] = (h * (1.0 + mod_ref[0, 2:3, :]) + mod_ref[0, 1:2, :]).astype(BF16)
            o_ref[rs, :] = partial_sum(rs)

    @pl.when((k > 0) & (k < last))
    def _():
        for m0 in range(0, h_scr.shape[0], MLP_SUB_M):
            rs = slice(m0, m0 + MLP_SUB_M)
            o_ref[rs, :] += partial_sum(rs)

    @pl.when(k == last)
    def _():
        for m0 in range(0, h_scr.shape[0], MLP_SUB_M):
            rs = slice(m0, m0 + MLP_SUB_M)
            y = o_ref[rs, :] + partial_sum(rs)
            o_ref[rs, :] = x1_ref[rs, :] + mod_ref[0, 3:4, :] * _rms_norm(y, postw_ref[...])


def _mlp(x1, mod3, prew, postw, w_up_bf, w_down_bf, tm, tf, rows_per_batch):
    m, d = x1.shape
    ff = w_up_bf.shape[1]
    assert ff // tf >= 2, "the first and last hidden chunks take different branches"
    tiles_per_batch = rows_per_batch // tm
    vec = pl.BlockSpec((1, d), lambda i, k: (0, 0))
    sub = min(tm, MLP_SUB_M)
    vmem_estimate = (2 * 2 * tm * d * 4 + tm * d * 2 + 2 * 2 * d * tf * 2
                     + sub * (tf * 4 + tf * 2 + d * 4) + VMEM_COMPILER_ALLOWANCE)
    return pl.pallas_call(
        _mlp_kernel,
        grid=(m // tm, ff // tf),
        in_specs=[pl.BlockSpec((tm, d), lambda i, k: (i, 0)),
                  pl.BlockSpec((1,) + mod3.shape[1:], lambda i, k: (i // tiles_per_batch, 0, 0)),
                  vec, vec,
                  pl.BlockSpec((d, tf), lambda i, k: (0, k)),
                  pl.BlockSpec((tf, d), lambda i, k: (k, 0))],
        out_specs=pl.BlockSpec((tm, d), lambda i, k: (i, 0)),
        out_shape=jax.ShapeDtypeStruct((m, d), F32),
        scratch_shapes=[pltpu.VMEM((tm, d), BF16)],
        compiler_params=_params(2, max(vmem_estimate, VMEM_LIMIT_BYTES)),
        name="mlp",
    )(x1, mod3, prew, postw, w_up_bf, w_down_bf)


def _tile(n, target):
    t = min(n, target)
    assert n % t == 0, (n, t)
    return t


def kernel(x, c, ctx, c_ctx, w_ada, b_ada, mix_pre_w, mix_post_w, mlp_pre_w, mlp_post_w, w_in, conv_w, conv_b,
           conv_ln_w, conv_ln_b, rec_lb_logits, rec_norm_w, w_out, w_up, w_down):
    bsz, seq, d = x.shape
    assert w_in.shape[0] == 1, "single-layer block"
    gw = d // 2
    assert gw % REC_DK == 0 and seq % REC_CHUNK == 0 and ctx.shape[1] % SUBLANES == 0
    m = bsz * seq

    cc = jnp.concatenate([c, c_ctx[None, :], jnp.zeros((SUBLANES - bsz - 1, d), c.dtype)], axis=0)
    mod_a = _ada(cc, w_ada[0], b_ada, _tile(2 * d, 1024), 2 * d).reshape(SUBLANES, 2, d)

    lbl = rec_lb_logits.astype(F32)

    s0f, s0b = _ctx_states(ctx, mod_a, mix_pre_w, lbl, w_in[0], col0=2 * gw + 2 * gw,
                           heads_per_step=min(4, gw // REC_DK))

    x2 = x.reshape(m, d)
    tm = _tile(seq, 512)
    z2 = _inproj(x2, mod_a, mix_pre_w, lbl, w_in[0], _tile(seq, 1024), _tile(gw, 512), seq)
    z3 = z2.reshape(bsz, seq, 6 * gw)

    tl = _tile(seq, 512)
    uconv, mod_b, (w_out_bf, w_up_bf, w_down_bf) = _conv(
        z3, conv_w[0], conv_b, conv_ln_w, conv_ln_b, tl, _tile(tl, 64), [w_out[0], w_up[0], w_down[0]],
        cc, w_ada[0], b_ada, ada_col0=2 * d)
    mod_b = mod_b.reshape(SUBLANES, 4, d)
    o_f, o_b = _gla(z3, s0f, s0b, _tile(seq, 512), gw)

    x1 = _outproj(uconv.reshape(m, gw), o_f.reshape(m, gw), o_b.reshape(m, gw), z2, x2, mod_b,
                  rec_norm_w, mix_post_w, w_out_bf, _tile(seq, 512), seq)
    out = _mlp(x1, mod_b, mlp_pre_w, mlp_post_w, w_up_bf, w_down_bf, _tile(seq, 1024), _tile(w_up.shape[2], 1024), seq)
    return out.reshape(bsz, seq, d)
```

```python
import functools

import jax
import jax.numpy as jnp
from jax import lax
from jax.experimental import pallas as pl
from jax.experimental.pallas import tpu as pltpu

NORM_EPS = 1e-6
REC_DK = 128
REC_CHUNK = 64
CONV_K = 31
CONV_HALO = 16
CONV_SLAB = 256
SUBLANES = 8
SUB_M = 256
PROJ_SUB_M = 512
MLP_SUB_M = 512
V7X_VMEM_BYTES = 64 * 1024 * 1024
VMEM_LIMIT_BYTES = 56 * 1024 * 1024
VMEM_COMPILER_ALLOWANCE = 3 * 1024 * 1024

F32 = jnp.float32
BF16 = jnp.bfloat16


def _params(n_axes, vmem_limit_bytes=VMEM_LIMIT_BYTES):
    assert vmem_limit_bytes < V7X_VMEM_BYTES
    return pltpu.CompilerParams(dimension_semantics=("arbitrary",) * n_axes,
                                vmem_limit_bytes=vmem_limit_bytes)


def _sigmoid(x):
    return 1.0 / (1.0 + jnp.exp(-x))


def _rms_norm(xf, w):
    ms = jnp.mean(xf * xf, axis=-1, keepdims=True)
    return xf * lax.rsqrt(ms + NORM_EPS) * w


def _dot(a, b):
    return jnp.dot(a, b, preferred_element_type=F32)


def _dot_nt(a, b):
    return lax.dot_general(a, b, (((1,), (1,)), ((), ())), preferred_element_type=F32)


def _dot_tn(a, b):
    return lax.dot_general(a, b, (((0,), (0,)), ((), ())), preferred_element_type=F32)


def _tri_sum(tri, g):
    if g.dtype == BF16:
        return _dot(tri, g)
    g_hi = g.astype(BF16)
    g_lo = (g - g_hi.astype(F32)).astype(BF16)
    return _dot(tri, g_hi) + _dot(tri, g_lo)


def _lower_bound(lbl, direction):
    m = jnp.max(lbl, axis=0)
    e = jnp.exp(lbl - m[None])
    sm0 = e[0] / jnp.sum(e, axis=0)
    return sm0[direction:direction + 1]


def _log_forget(z, lb):
    return jnp.log2(lb + (1.0 - lb) * _sigmoid(z))


def _ada_kernel(c_ref, w_ref, b_ref, o_ref):
    c = c_ref[...]
    s = (c * _sigmoid(c)).astype(BF16)
    o_ref[...] = _dot(s, w_ref[...].astype(BF16)) + b_ref[...]


def _ada(cc, w_ada, b_ada, tn, n_cols):
    rows, d = cc.shape
    return pl.pallas_call(
        _ada_kernel,
        grid=(n_cols // tn,),
        in_specs=[pl.BlockSpec((rows, d), lambda j: (0, 0)),
                  pl.BlockSpec((d, tn), lambda j: (0, j)),
                  pl.BlockSpec((1, tn), lambda j: (0, j))],
        out_specs=pl.BlockSpec((rows, tn), lambda j: (0, j)),
        out_shape=jax.ShapeDtypeStruct((rows, n_cols), F32),
        compiler_params=_params(1),
        name="ada",
    )(cc, w_ada, b_ada)


def _ctx_kernel(ctx_ref, mod_ref, npw_ref, lbl_ref, wi_ref, wf_ref, wb_ref, sf_ref, sb_ref, h_scr):
    @pl.when(pl.program_id(1) == 0)
    def _():
        h = _rms_norm(ctx_ref[0], npw_ref[...])
        h_scr[...] = (h * (1.0 + mod_ref[0, 1:2, :]) + mod_ref[0, 0:1, :]).astype(BF16)

    hb = h_scr[...]
    lc = hb.shape[0]
    v = _dot(hb, wi_ref[...].astype(BF16)).astype(BF16)
    lbl = lbl_ref[...]
    g_f = _log_forget(_dot(hb, wf_ref[...].astype(BF16)), _lower_bound(lbl, 0))
    g_b = _log_forget(_dot(hb, wb_ref[...].astype(BF16)), _lower_bound(lbl, 1))
    row = lax.broadcasted_iota(jnp.int32, (lc, lc), 0)
    col = lax.broadcasted_iota(jnp.int32, (lc, lc), 1)
    incl = (col <= row).astype(BF16)
    excl = (col < row).astype(BF16)
    b_f = _tri_sum(incl, g_f)
    kd_f = ((1.0 - jnp.exp2(g_f)) * jnp.exp2(b_f[lc - 1:lc, :] - b_f)).astype(BF16)
    c_b = _tri_sum(excl, g_b)
    kd_b = ((1.0 - jnp.exp2(g_b)) * jnp.exp2(c_b)).astype(BF16)
    for h in range(v.shape[1] // REC_DK):
        ls = slice(h * REC_DK, (h + 1) * REC_DK)
        sf_ref[0, h] = _dot_tn(v[:, ls], kd_f[:, ls])
        sb_ref[0, h] = _dot_tn(v[:, ls], kd_b[:, ls])


def _ctx_states(ctx, mod3, npw, lbl, w_in, col0, heads_per_step):
    bsz, lc, d = ctx.shape
    nh = lbl.shape[-1] // REC_DK
    hg = heads_per_step
    wcols = hg * REC_DK
    cb = col0 // wcols
    ng = nh // hg
    wspec = lambda off: pl.BlockSpec((d, wcols), lambda b, g, off=off: (0, cb + off * ng + g))
    st_spec = pl.BlockSpec((1, hg, REC_DK, REC_DK), lambda b, g: (b, g, 0, 0))
    st_shape = jax.ShapeDtypeStruct((bsz, nh, REC_DK, REC_DK), F32)
    return pl.pallas_call(
        _ctx_kernel,
        grid=(bsz, ng),
        in_specs=[pl.BlockSpec((1, lc, d), lambda b, g: (b, 0, 0)),
                  pl.BlockSpec((1,) + mod3.shape[1:], lambda b, g: (bsz, 0, 0)),
                  pl.BlockSpec((1, d), lambda b, g: (0, 0)),
                  pl.BlockSpec(lbl.shape[:2] + (wcols,), lambda b, g: (0, 0, g)),
                  wspec(0), wspec(1), wspec(2)],
        out_specs=[st_spec, st_spec],
        out_shape=[st_shape, st_shape],
        scratch_shapes=[pltpu.VMEM((lc, d), BF16)],
        compiler_params=_params(2),
        name="ctx_states",
    )(ctx, mod3, npw, lbl, w_in, w_in, w_in)


def _project_tiles(h_scr, w_ref, wb_scr, tn, epilogue):
    for m0 in range(0, h_scr.shape[0], PROJ_SUB_M):
        rs = slice(m0, m0 + PROJ_SUB_M)
        for n0 in range(0, w_ref.shape[1], tn):
            cs = slice(n0, n0 + tn)
            if m0 == 0:
                wb_scr[:, cs] = w_ref[:, cs].astype(BF16)
            epilogue(rs, cs, _dot(h_scr[rs, :], wb_scr[:, cs]))


def _inproj_kernel(x_ref, mod_ref, npw_ref, lbl_ref, w_ref, o_ref, h_scr, a_scr, wb_scr, *, tn):
    j = pl.program_id(1)

    @pl.when(j == 0)
    def _():
        h = _rms_norm(x_ref[...], npw_ref[...])
        h_scr[...] = (h * (1.0 + mod_ref[0, 1:2, :]) + mod_ref[0, 0:1, :]).astype(BF16)

        def park(rs, cs, z):
            a_scr[rs, cs] = z
        _project_tiles(h_scr, w_ref, wb_scr, tn, park)

    @pl.when(j == 1)
    def _():
        def glu(rs, cs, z):
            o_ref[rs, cs] = (a_scr[rs, cs] * _sigmoid(z)).astype(o_ref.dtype)
        _project_tiles(h_scr, w_ref, wb_scr, tn, glu)

    @pl.when((j == 2) | (j == 3))
    def _():
        def swish(rs, cs, z):
            o_ref[rs, cs] = (z * _sigmoid(z)).astype(o_ref.dtype)
        _project_tiles(h_scr, w_ref, wb_scr, tn, swish)

    @pl.when(j == 4)
    def _():
        def values(rs, cs, z):
            o_ref[rs, cs] = z.astype(o_ref.dtype)
        _project_tiles(h_scr, w_ref, wb_scr, tn, values)

    @pl.when(j >= 5)
    def _():
        lbl = lbl_ref[...]
        lb = jnp.where(j == 5, _lower_bound(lbl, 0), _lower_bound(lbl, 1))

        def log_gate(rs, cs, z):
            o_ref[rs, cs] = _log_forget(z, lb[:, cs]).astype(o_ref.dtype)
        _project_tiles(h_scr, w_ref, wb_scr, tn, log_gate)


def _inproj(x2, mod3, npw, lbl, w_in, tm, tn, rows_per_batch):
    m, d = x2.shape
    gw = d // 2
    n_groups = w_in.shape[1] // gw
    tiles_per_batch = rows_per_batch // tm
    return pl.pallas_call(
        functools.partial(_inproj_kernel, tn=tn),
        grid=(m // tm, n_groups),
        in_specs=[pl.BlockSpec((tm, d), lambda i, j: (i, 0)),
                  pl.BlockSpec((1,) + mod3.shape[1:], lambda i, j: (i // tiles_per_batch, 0, 0)),
                  pl.BlockSpec((1, d), lambda i, j: (0, 0)),
                  pl.BlockSpec(lbl.shape, lambda i, j: (0, 0, 0)),
                  pl.BlockSpec((d, gw), lambda i, j: (0, j))],
        out_specs=pl.BlockSpec((tm, gw), lambda i, j: (i, jnp.maximum(j - 1, 0))),
        out_shape=jax.ShapeDtypeStruct((m, (n_groups - 1) * gw), BF16),
        scratch_shapes=[pltpu.VMEM((tm, d), BF16), pltpu.VMEM((tm, gw), F32), pltpu.VMEM((d, gw), BF16)],
        compiler_params=_params(2),
        name="inproj",
    )(x2, mod3, npw, lbl, w_in)


def _conv_kernel(u_ref, ul_ref, ur_ref, cw_ref, cb_ref, lnw_ref, lnb_ref, cc_ref, wada_ref, bada_ref, *rest,
                 rb, n_cast):
    cast_in, o_ref, modb_ref = rest[:n_cast], rest[n_cast], rest[n_cast + 1]
    cast_out = rest[n_cast + 2:2 * n_cast + 2]
    buf, y_scr = rest[2 * n_cast + 2:]
    for src, dst in zip(cast_in, cast_out):
        dst[...] = src[...].astype(dst.dtype)
    _ada_kernel(cc_ref, wada_ref, bada_ref, modb_ref)

    t = pl.program_id(1)
    tl, ch = u_ref.shape[1], u_ref.shape[2]
    buf[pl.ds(CONV_HALO, tl), :] = u_ref[0]
    buf[pl.ds(0, CONV_HALO), :] = jnp.where(t > 0, ul_ref[0], jnp.zeros_like(ul_ref[0]))
    buf[pl.ds(CONV_HALO + tl, CONV_HALO), :] = jnp.where(t < pl.num_programs(1) - 1, ur_ref[0],
                                                          jnp.zeros_like(ur_ref[0]))

    shift0 = CONV_HALO - CONV_K // 2
    win = rb + 2 * CONV_HALO
    span = win - SUBLANES
    r_i = lax.broadcasted_iota(jnp.int32, (span, win), 0)
    c_i = lax.broadcasted_iota(jnp.int32, (span, win), 1)
    shifts = [(c_i == r_i + p).astype(BF16) for p in range(SUBLANES)]

    def row_block(i, carry):
        r0 = pl.multiple_of(i * rb, rb)

        for l0 in range(0, ch, CONV_SLAB):
            ls = slice(l0, l0 + CONV_SLAB)
            w = buf[pl.ds(r0, win), ls]
            acc = jnp.zeros((rb, CONV_SLAB), F32)
            for p in range(SUBLANES):
                sp = _dot(shifts[p], w)
                for k in range(CONV_K):
                    if (k + shift0) % SUBLANES == p:
                        a0 = (k + shift0) // SUBLANES * SUBLANES
                        acc = acc + sp[a0:a0 + rb, :] * cw_ref[k:k + 1, ls]
            y_scr[pl.ds(r0, rb), ls] = acc
        return carry

    lax.fori_loop(0, tl // rb, row_block, 0, unroll=True)

    y = y_scr[...] + cb_ref[...]
    mu = jnp.mean(y, axis=-1, keepdims=True)
    yc = y - mu
    var = jnp.mean(yc * yc, axis=-1, keepdims=True)
    yn = yc * lax.rsqrt(var + NORM_EPS) * lnw_ref[...] + lnb_ref[...]
    o_ref[0] = (yn * _sigmoid(yn)).astype(o_ref.dtype)


def _conv(z3, conv_w, conv_b, ln_w, ln_b, tl, rb, cast_weights, cc, w_ada, b_ada, ada_col0):
    bsz, seq, _ = z3.shape
    ch = conv_w.shape[1]
    hb = tl // CONV_HALO
    n_halo = seq // CONV_HALO
    nt = seq // tl
    n_steps = bsz * nt
    vec = pl.BlockSpec((1, ch), lambda b, t: (0, 0))
    for w in cast_weights:
        assert w.shape[0] % (n_steps * 16) == 0, w.shape
    cast_specs = [pl.BlockSpec((w.shape[0] // n_steps, w.shape[1]), lambda b, t: (b * nt + t, 0))
                  for w in cast_weights]
    n_ada = w_ada.shape[1] - ada_col0
    ta = n_ada // n_steps
    assert n_ada % n_steps == 0 and ta % 128 == 0 and ada_col0 % ta == 0
    ada_specs = [pl.BlockSpec(cc.shape, lambda b, t: (0, 0)),
                 pl.BlockSpec((w_ada.shape[0], ta), lambda b, t: (0, ada_col0 // ta + b * nt + t)),
                 pl.BlockSpec((1, ta), lambda b, t: (0, ada_col0 // ta + b * nt + t))]
    outs = pl.pallas_call(
        functools.partial(_conv_kernel, rb=rb, n_cast=len(cast_weights)),
        grid=(bsz, nt),
        in_specs=[pl.BlockSpec((1, tl, ch), lambda b, t: (b, t, 0)),
                  pl.BlockSpec((1, CONV_HALO, ch), lambda b, t: (b, jnp.maximum(t * hb - 1, 0), 0)),
                  pl.BlockSpec((1, CONV_HALO, ch), lambda b, t: (b, jnp.minimum((t + 1) * hb, n_halo - 1), 0)),
                  pl.BlockSpec((CONV_K, ch), lambda b, t: (0, 0)),
                  vec, vec, vec] + ada_specs + cast_specs,
        out_specs=[pl.BlockSpec((1, tl, ch), lambda b, t: (b, t, 0)),
                   pl.BlockSpec((cc.shape[0], ta), lambda b, t: (0, b * nt + t))] + cast_specs,
        out_shape=[jax.ShapeDtypeStruct((bsz, seq, ch), BF16), jax.ShapeDtypeStruct((cc.shape[0], n_ada), F32)]
                  + [jax.ShapeDtypeStruct(w.shape, BF16) for w in cast_weights],
        scratch_shapes=[pltpu.VMEM((tl + 2 * CONV_HALO, ch), BF16), pltpu.VMEM((tl, ch), F32)],
        compiler_params=_params(2),
        name="conv",
    )(z3, z3, z3, conv_w, conv_b, ln_w, ln_b, cc, w_ada, b_ada, *cast_weights)
    return outs[0], outs[1], outs[2:]


class _Scan:
    def __init__(self, q_ref, v_ref, g_ref, o_ref, st, scr, backward):
        self.q_ref, self.v_ref, self.g_ref, self.o_ref, self.st = q_ref, v_ref, g_ref, o_ref, st
        self.kd_s, self.qa_s, self.ka_s, self.qe_s, self.dec_s, self.kv_s, self.sb_s = scr
        rows, width = q_ref.shape[1], q_ref.shape[2]
        self.n_chunks, self.n_heads = rows // REC_CHUNK, width // REC_DK
        r_i = lax.broadcasted_iota(jnp.int32, (rows, rows), 0)
        c_i = lax.broadcasted_iota(jnp.int32, (rows, rows), 1)
        same_chunk = (r_i // REC_CHUNK) == (c_i // REC_CHUNK)
        tri = (same_chunk & ((c_i >= r_i) if backward else (c_i <= r_i))).astype(BF16)
        m_r = lax.broadcasted_iota(jnp.int32, (REC_CHUNK, REC_CHUNK), 0)
        m_c = lax.broadcasted_iota(jnp.int32, (REC_CHUNK, REC_CHUNK), 1)
        self.mask = (m_c >= m_r) if backward else (m_c <= m_r)
        ref_f = REC_CHUNK // 2 - 1
        self.last, self.ref = (0, REC_CHUNK - 1 - ref_f) if backward else (REC_CHUNK - 1, ref_f)
        self.order = list(range(self.n_chunks - 1, -1, -1) if backward else range(self.n_chunks))
        self.b = _tri_sum(tri, g_ref[0])
        self.scores = {}

    def _rows(self, c):
        return slice(c * REC_CHUNK, (c + 1) * REC_CHUNK)

    def _heads(self):
        return [(h, slice(h * REC_DK, (h + 1) * REC_DK)) for h in range(self.n_heads)]

    def prep(self, c):
        rs = self._rows(c)
        bc, qc = self.b[rs], self.q_ref[0, rs, :].astype(F32)
        b_last = bc[self.last:self.last + 1, :]
        b_ref = bc[self.ref:self.ref + 1, :]
        k = 1.0 - jnp.exp2(self.g_ref[0, rs, :].astype(F32))
        from_ref = jnp.exp2(bc - b_ref)
        ka = k * (1.0 / from_ref)
        qa = qc * from_ref
        self.ka_s[rs, :] = ka.astype(BF16)
        self.qa_s[rs, :] = qa.astype(BF16)
        self.kd_s[rs, :] = (ka * jnp.exp2(b_last - b_ref)).astype(BF16)
        self.qe_s[rs, :] = (qa * jnp.exp2(b_ref)).astype(BF16)
        self.dec_s[c:c + 1, :] = jnp.exp2(b_last)

    def intra(self, c):
        rs = self._rows(c)
        for h, ls in self._heads():
            s = _dot_nt(self.qa_s[rs, ls], self.ka_s[rs, ls])
            self.scores[c, h] = jnp.where(self.mask, s, 0.0).astype(BF16)
            self.kv_s[c, h] = _dot_tn(self.v_ref[0, rs, ls], self.kd_s[rs, ls])

    def recur(self, c):
        for h, ls in self._heads():
            s = self.st[h]
            self.sb_s[c, h] = s.astype(BF16)
            self.st[h] = self.dec_s[c:c + 1, ls] * s + self.kv_s[c, h]

    def out(self, c):
        rs = self._rows(c)
        for h, ls in self._heads():
            self.o_ref[0, rs, ls] = (_dot(self.scores[c, h], self.v_ref[0, rs, ls])
                                     + _dot_nt(self.qe_s[rs, ls], self.sb_s[c, h]))


def _gla_kernel(qf_ref, vf_ref, gf_ref, qb_ref, vb_ref, gb_ref, s0f_ref, s0b_ref,
                of_ref, ob_ref, stf, stb, *scr):
    @pl.when(pl.program_id(1) == 0)
    def _():
        stf[...] = s0f_ref[0]
        stb[...] = s0b_ref[0]

    half = len(scr) // 2
    fwd = _Scan(qf_ref, vf_ref, gf_ref, of_ref, stf, scr[:half], backward=False)
    bwd = _Scan(qb_ref, vb_ref, gb_ref, ob_ref, stb, scr[half:], backward=True)
    for cf, cb in zip(fwd.order, bwd.order):
        for scan, c in ((fwd, cf), (bwd, cb)):
            scan.prep(c)
        for scan, c in ((fwd, cf), (bwd, cb)):
            scan.intra(c)
        for scan, c in ((fwd, cf), (bwd, cb)):
            scan.recur(c)
            scan.out(c)


def _gla(z3, s0f, s0b, rows, width):
    bsz, seq, _ = z3.shape
    nb = seq // rows
    nh = width // REC_DK
    nc = rows // REC_CHUNK
    state = pltpu.VMEM((nh, REC_DK, REC_DK), F32)
    per_direction = ([pltpu.VMEM((rows, width), BF16)] * 4
                     + [pltpu.VMEM((max(nc, SUBLANES), width), F32),
                        pltpu.VMEM((nc, nh, REC_DK, REC_DK), F32),
                        pltpu.VMEM((nc, nh, REC_DK, REC_DK), BF16)])
    fwd = lambda col: pl.BlockSpec((1, rows, width), lambda b, n, col=col: (b, n, col))
    bwd = lambda col: pl.BlockSpec((1, rows, width), lambda b, n, col=col: (b, nb - 1 - n, col))
    st_spec = pl.BlockSpec((1, nh, REC_DK, REC_DK), lambda b, n: (b, 0, 0, 0))
    o_shape = jax.ShapeDtypeStruct((bsz, seq, width), F32)
    return pl.pallas_call(
        _gla_kernel,
        grid=(bsz, nb),
        in_specs=[fwd(1), fwd(3), fwd(4), bwd(1), bwd(3), bwd(5), st_spec, st_spec],
        out_specs=[pl.BlockSpec((1, rows, width), lambda b, n: (b, n, 0)),
                   pl.BlockSpec((1, rows, width), lambda b, n: (b, nb - 1 - n, 0))],
        out_shape=[o_shape, o_shape],
        scratch_shapes=[state, state] + per_direction * 2,
        compiler_params=_params(2),
        name="gla",
    )(z3, z3, z3, z3, z3, z3, s0f, s0b)


def _outproj_kernel(u_ref, of_ref, ob_ref, gs_ref, x_ref, mod_ref, rnw_ref, postw_ref, w_ref, x1_ref):
    gw = u_ref.shape[1]
    o = of_ref[...] + ob_ref[...]
    heads = []
    for h in range(gw // REC_DK):
        oh = o[:, h * REC_DK:(h + 1) * REC_DK]
        heads.append(oh * lax.rsqrt(jnp.mean(oh * oh, axis=-1, keepdims=True) + NORM_EPS))
    on = jnp.concatenate(heads, axis=-1) * rnw_ref[...] * gs_ref[...]
    y = _dot(u_ref[...].astype(BF16), w_ref[0:gw, :]) + _dot(on.astype(BF16), w_ref[gw:2 * gw, :])
    x1_ref[...] = x_ref[...] + mod_ref[0, 0:1, :] * _rms_norm(y, postw_ref[...])


def _outproj(uconv, o_f, o_b, z2, x2, mod3, rnw, postw, w_out_bf, tm, rows_per_batch):
    m, d = x2.shape
    gw = d // 2
    tiles_per_batch = rows_per_batch // tm
    half = pl.BlockSpec((tm, gw), lambda i: (i, 0))
    full = pl.BlockSpec((tm, d), lambda i: (i, 0))
    vec = lambda n: pl.BlockSpec((1, n), lambda i: (0, 0))
    return pl.pallas_call(
        _outproj_kernel,
        grid=(m // tm,),
        in_specs=[half, half, half,
                  pl.BlockSpec((tm, gw), lambda i: (i, 2)),
                  full,
                  pl.BlockSpec((1,) + mod3.shape[1:], lambda i: (i // tiles_per_batch, 0, 0)),
                  vec(gw), vec(d),
                  pl.BlockSpec((d, d), lambda i: (0, 0))],
        out_specs=full,
        out_shape=jax.ShapeDtypeStruct((m, d), F32),
        compiler_params=_params(1),
        name="outproj",
    )(uconv, o_f, o_b, z2, x2, mod3, rnw, postw, w_out_bf)


def _mlp_kernel(x1_ref, mod_ref, prew_ref, postw_ref, wu_ref, wd_ref, o_ref, h_scr):
    k = pl.program_id(1)
    last = pl.num_programs(1) - 1

    def partial_sum(rs):
        hid = jnp.maximum(_dot(h_scr[rs, :], wu_ref[...]), 0.0)
        return _dot((hid * hid).astype(BF16), wd_ref[...])

    @pl.when(k == 0)
    def _():
        for m0 in range(0, h_scr.shape[0], MLP_SUB_M):
            rs = slice(m0, m0 + MLP_SUB_M)
            h = _rms_norm(x1_ref[rs, :], prew_ref[...])
            h_scr[rs, :] = (h * (1.0 + mod_ref[0, 2:3, :]) + mod_ref[0, 1:2, :]).astype(BF16)
            o_ref[rs, :] = partial_sum(rs)

    @pl.when((k > 0) & (k < last))
    def _():
        for m0 in range(0, h_scr.shape[0], MLP_SUB_M):
            rs = slice(m0, m0 + MLP_SUB_M)
            o_ref[rs, :] += partial_sum(rs)

    @pl.when(k == last)
    def _():
        for m0 in range(0, h_scr.shape[0], MLP_SUB_M):
            rs = slice(m0, m0 + MLP_SUB_M)
            y = o_ref[rs, :] + partial_sum(rs)
            o_ref[rs, :] = x1_ref[rs, :] + mod_ref[0, 3:4, :] * _rms_norm(y, postw_ref[...])


def _mlp(x1, mod3, prew, postw, w_up_bf, w_down_bf, tm, tf, rows_per_batch):
    m, d = x1.shape
    ff = w_up_bf.shape[1]
    assert ff // tf >= 2, "the first and last hidden chunks take different branches"
    tiles_per_batch = rows_per_batch // tm
    vec = pl.BlockSpec((1, d), lambda i, k: (0, 0))
    sub = min(tm, MLP_SUB_M)
    vmem_estimate = (2 * 2 * tm * d * 4 + tm * d * 2 + 2 * 2 * d * tf * 2
                     + sub * (tf * 4 + tf * 2 + d * 4) + VMEM_COMPILER_ALLOWANCE)
    return pl.pallas_call(
        _mlp_kernel,
        grid=(m // tm, ff // tf),
        in_specs=[pl.BlockSpec((tm, d), lambda i, k: (i, 0)),
                  pl.BlockSpec((1,) + mod3.shape[1:], lambda i, k: (i // tiles_per_batch, 0, 0)),
                  vec, vec,
                  pl.BlockSpec((d, tf), lambda i, k: (0, k)),
                  pl.BlockSpec((tf, d), lambda i, k: (k, 0))],
        out_specs=pl.BlockSpec((tm, d), lambda i, k: (i, 0)),
        out_shape=jax.ShapeDtypeStruct((m, d), F32),
        scratch_shapes=[pltpu.VMEM((tm, d), BF16)],
        compiler_params=_params(2, max(vmem_estimate, VMEM_LIMIT_BYTES)),
        name="mlp",
    )(x1, mod3, prew, postw, w_up_bf, w_down_bf)


def _tile(n, target):
    t = min(n, target)
    assert n % t == 0, (n, t)
    return t


def kernel(x, c, ctx, c_ctx, w_ada, b_ada, mix_pre_w, mix_post_w, mlp_pre_w, mlp_post_w, w_in, conv_w, conv_b,
           conv_ln_w, conv_ln_b, rec_lb_logits, rec_norm_w, w_out, w_up, w_down):
    bsz, seq, d = x.shape
    assert w_in.shape[0] == 1, "single-layer block"
    gw = d // 2
    assert gw % REC_DK == 0 and seq % REC_CHUNK == 0 and ctx.shape[1] % SUBLANES == 0
    m = bsz * seq

    cc = jnp.concatenate([c, c_ctx[None, :], jnp.zeros((SUBLANES - bsz - 1, d), c.dtype)], axis=0)
    mod_a = _ada(cc, w_ada[0], b_ada, _tile(2 * d, 1024), 2 * d).reshape(SUBLANES, 2, d)

    lbl = rec_lb_logits.astype(F32)

    s0f, s0b = _ctx_states(ctx, mod_a, mix_pre_w, lbl, w_in[0], col0=2 * gw + 2 * gw,
                           heads_per_step=min(4, gw // REC_DK))

    x2 = x.reshape(m, d)
    tm = _tile(seq, 512)
    z2 = _inproj(x2, mod_a, mix_pre_w, lbl, w_in[0], _tile(seq, 1024), _tile(gw, 512), seq)
    z3 = z2.reshape(bsz, seq, 6 * gw)

    tl = _tile(seq, 512)
    uconv, mod_b, (w_out_bf, w_up_bf, w_down_bf) = _conv(
        z3, conv_w[0], conv_b, conv_ln_w, conv_ln_b, tl, _tile(tl, 64), [w_out[0], w_up[0], w_down[0]],
        cc, w_ada[0], b_ada, ada_col0=2 * d)
    mod_b = mod_b.reshape(SUBLANES, 4, d)
    o_f, o_b = _gla(z3, s0f, s0b, _tile(seq, 512), gw)

    x1 = _outproj(uconv.reshape(m, gw), o_f.reshape(m, gw), o_b.reshape(m, gw), z2, x2, mod_b,
                  rec_norm_w, mix_post_w, w_out_bf, _tile(seq, 512), seq)
    out = _mlp(x1, mod_b, mlp_pre_w, mlp_post_w, w_up_bf, w_down_bf, _tile(seq, 1024), _tile(w_up.shape[2], 1024), seq)
    return out.reshape(bsz, seq, d)
```

```python
import functools

import jax
import jax.numpy as jnp
from jax import lax
from jax.experimental import pallas as pl
from jax.experimental.pallas import tpu as pltpu

NORM_EPS = 1e-6
REC_DK = 128
REC_CHUNK = 64
CONV_K = 31
CONV_HALO = 16
CONV_SLAB = 256
SUBLANES = 8
PROJ_SUB_M = 512
MLP_SUB_M = 512
V7X_VMEM_BYTES = 64 * 1024 * 1024
VMEM_LIMIT_BYTES = 56 * 1024 * 1024
VMEM_COMPILER_ALLOWANCE = 3 * 1024 * 1024

F32 = jnp.float32
BF16 = jnp.bfloat16


def _params(n_axes, vmem_limit_bytes=VMEM_LIMIT_BYTES):
    assert vmem_limit_bytes < V7X_VMEM_BYTES
    return pltpu.CompilerParams(dimension_semantics=("arbitrary",) * n_axes,
                                vmem_limit_bytes=vmem_limit_bytes)


def _row_tiles(rows, sub):
    sub = min(rows, sub)
    assert rows % sub == 0, (rows, sub)
    return [slice(r0, r0 + sub) for r0 in range(0, rows, sub)]


def _sigmoid(x):
    return 1.0 / (1.0 + jnp.exp(-x))


def _rms_norm(xf, w):
    ms = jnp.mean(xf * xf, axis=-1, keepdims=True)
    return xf * lax.rsqrt(ms + NORM_EPS) * w


def _dot(a, b):
    return jnp.dot(a, b, preferred_element_type=F32)


def _dot_nt(a, b):
    return lax.dot_general(a, b, (((1,), (1,)), ((), ())), preferred_element_type=F32)


def _dot_tn(a, b):
    return lax.dot_general(a, b, (((0,), (0,)), ((), ())), preferred_element_type=F32)


def _tri_sum(tri, g):
    if g.dtype == BF16:
        return _dot(tri, g)
    g_hi = g.astype(BF16)
    g_lo = (g - g_hi.astype(F32)).astype(BF16)
    return _dot(tri, g_hi) + _dot(tri, g_lo)


def _lower_bound(lbl, direction):
    m = jnp.max(lbl, axis=0)
    e = jnp.exp(lbl - m[None])
    sm0 = e[0] / jnp.sum(e, axis=0)
    return sm0[direction:direction + 1]


def _log_forget(z, lb):
    return jnp.log2(lb + (1.0 - lb) * _sigmoid(z))


def _ada_kernel(c_ref, w_ref, b_ref, o_ref):
    c = c_ref[...]
    s = (c * _sigmoid(c)).astype(BF16)
    o_ref[...] = _dot(s, w_ref[...].astype(BF16)) + b_ref[...]


def _ada(cc, w_ada, b_ada, tn, n_cols):
    rows, d = cc.shape
    return pl.pallas_call(
        _ada_kernel,
        grid=(n_cols // tn,),
        in_specs=[pl.BlockSpec((rows, d), lambda j: (0, 0)),
                  pl.BlockSpec((d, tn), lambda j: (0, j)),
                  pl.BlockSpec((1, tn), lambda j: (0, j))],
        out_specs=pl.BlockSpec((rows, tn), lambda j: (0, j)),
        out_shape=jax.ShapeDtypeStruct((rows, n_cols), F32),
        compiler_params=_params(1),
        name="ada",
    )(cc, w_ada, b_ada)


def _ctx_kernel(ctx_ref, mod_ref, npw_ref, lbl_ref, wi_ref, wf_ref, wb_ref, sf_ref, sb_ref, h_scr):
    @pl.when(pl.program_id(1) == 0)
    def _():
        h = _rms_norm(ctx_ref[0], npw_ref[...])
        h_scr[...] = (h * (1.0 + mod_ref[0, 1:2, :]) + mod_ref[0, 0:1, :]).astype(BF16)

    hb = h_scr[...]
    lc = hb.shape[0]
    v = _dot(hb, wi_ref[...].astype(BF16)).astype(BF16)
    lbl = lbl_ref[...]
    g_f = _log_forget(_dot(hb, wf_ref[...].astype(BF16)), _lower_bound(lbl, 0))
    g_b = _log_forget(_dot(hb, wb_ref[...].astype(BF16)), _lower_bound(lbl, 1))
    row = lax.broadcasted_iota(jnp.int32, (lc, lc), 0)
    col = lax.broadcasted_iota(jnp.int32, (lc, lc), 1)
    incl = (col <= row).astype(BF16)
    excl = (col < row).astype(BF16)
    b_f = _tri_sum(incl, g_f)
    kd_f = ((1.0 - jnp.exp2(g_f)) * jnp.exp2(b_f[lc - 1:lc, :] - b_f)).astype(BF16)
    c_b = _tri_sum(excl, g_b)
    kd_b = ((1.0 - jnp.exp2(g_b)) * jnp.exp2(c_b)).astype(BF16)
    for h in range(v.shape[1] // REC_DK):
        ls = slice(h * REC_DK, (h + 1) * REC_DK)
        sf_ref[0, h] = _dot_tn(v[:, ls], kd_f[:, ls])
        sb_ref[0, h] = _dot_tn(v[:, ls], kd_b[:, ls])


def _ctx_states(ctx, mod3, npw, lbl, w_in, col0, heads_per_step):
    bsz, lc, d = ctx.shape
    nh = lbl.shape[-1] // REC_DK
    hg = heads_per_step
    wcols = hg * REC_DK
    cb = col0 // wcols
    ng = nh // hg
    wspec = lambda off: pl.BlockSpec((d, wcols), lambda b, g, off=off: (0, cb + off * ng + g))
    st_spec = pl.BlockSpec((1, hg, REC_DK, REC_DK), lambda b, g: (b, g, 0, 0))
    st_shape = jax.ShapeDtypeStruct((bsz, nh, REC_DK, REC_DK), F32)
    return pl.pallas_call(
        _ctx_kernel,
        grid=(bsz, ng),
        in_specs=[pl.BlockSpec((1, lc, d), lambda b, g: (b, 0, 0)),
                  pl.BlockSpec((1,) + mod3.shape[1:], lambda b, g: (bsz, 0, 0)),
                  pl.BlockSpec((1, d), lambda b, g: (0, 0)),
                  pl.BlockSpec(lbl.shape[:2] + (wcols,), lambda b, g: (0, 0, g)),
                  wspec(0), wspec(1), wspec(2)],
        out_specs=[st_spec, st_spec],
        out_shape=[st_shape, st_shape],
        scratch_shapes=[pltpu.VMEM((lc, d), BF16)],
        compiler_params=_params(2),
        name="ctx_states",
    )(ctx, mod3, npw, lbl, w_in, w_in, w_in)


def _project_tiles(h_scr, w_ref, wb_scr, tn, epilogue):
    for rs in _row_tiles(h_scr.shape[0], PROJ_SUB_M):
        for n0 in range(0, w_ref.shape[1], tn):
            cs = slice(n0, n0 + tn)
            if rs.start == 0:
                wb_scr[:, cs] = w_ref[:, cs].astype(BF16)
            epilogue(rs, cs, _dot(h_scr[rs, :], wb_scr[:, cs]))


def _inproj_kernel(x_ref, mod_ref, npw_ref, lbl_ref, w_ref, o_ref, h_scr, a_scr, wb_scr, *, tn):
    j = pl.program_id(1)

    @pl.when(j == 0)
    def _():
        h = _rms_norm(x_ref[...], npw_ref[...])
        h_scr[...] = (h * (1.0 + mod_ref[0, 1:2, :]) + mod_ref[0, 0:1, :]).astype(BF16)

        def park(rs, cs, z):
            a_scr[rs, cs] = z
        _project_tiles(h_scr, w_ref, wb_scr, tn, park)

    @pl.when(j == 1)
    def _():
        def glu(rs, cs, z):
            o_ref[rs, cs] = (a_scr[rs, cs] * _sigmoid(z)).astype(o_ref.dtype)
        _project_tiles(h_scr, w_ref, wb_scr, tn, glu)

    @pl.when((j == 2) | (j == 3))
    def _():
        def swish(rs, cs, z):
            o_ref[rs, cs] = (z * _sigmoid(z)).astype(o_ref.dtype)
        _project_tiles(h_scr, w_ref, wb_scr, tn, swish)

    @pl.when(j == 4)
    def _():
        def values(rs, cs, z):
            o_ref[rs, cs] = z.astype(o_ref.dtype)
        _project_tiles(h_scr, w_ref, wb_scr, tn, values)

    @pl.when(j >= 5)
    def _():
        lbl = lbl_ref[...]
        lb = jnp.where(j == 5, _lower_bound(lbl, 0), _lower_bound(lbl, 1))

        def log_gate(rs, cs, z):
            o_ref[rs, cs] = _log_forget(z, lb[:, cs]).astype(o_ref.dtype)
        _project_tiles(h_scr, w_ref, wb_scr, tn, log_gate)


def _inproj(x2, mod3, npw, lbl, w_in, tm, tn, rows_per_batch):
    m, d = x2.shape
    gw = d // 2
    n_groups = w_in.shape[1] // gw
    tiles_per_batch = rows_per_batch // tm
    return pl.pallas_call(
        functools.partial(_inproj_kernel, tn=tn),
        grid=(m // tm, n_groups),
        in_specs=[pl.BlockSpec((tm, d), lambda i, j: (i, 0)),
                  pl.BlockSpec((1,) + mod3.shape[1:], lambda i, j: (i // tiles_per_batch, 0, 0)),
                  pl.BlockSpec((1, d), lambda i, j: (0, 0)),
                  pl.BlockSpec(lbl.shape, lambda i, j: (0, 0, 0)),
                  pl.BlockSpec((d, gw), lambda i, j: (0, j))],
        out_specs=pl.BlockSpec((None, tm, gw), lambda i, j: (jnp.maximum(j - 1, 0), i, 0)),
        out_shape=jax.ShapeDtypeStruct((n_groups - 1, m, gw), BF16),
        scratch_shapes=[pltpu.VMEM((tm, d), BF16), pltpu.VMEM((tm, gw), F32), pltpu.VMEM((d, gw), BF16)],
        compiler_params=_params(2),
        name="inproj",
    )(x2, mod3, npw, lbl, w_in)


def _conv_kernel(u_ref, ul_ref, ur_ref, cw_ref, cb_ref, lnw_ref, lnb_ref, cc_ref, wada_ref, bada_ref, *rest,
                 rb, n_cast):
    cast_in, o_ref, modb_ref = rest[:n_cast], rest[n_cast], rest[n_cast + 1]
    cast_out = rest[n_cast + 2:2 * n_cast + 2]
    buf, y_scr = rest[2 * n_cast + 2:]
    for src, dst in zip(cast_in, cast_out):
        dst[...] = src[...].astype(dst.dtype)
    _ada_kernel(cc_ref, wada_ref, bada_ref, modb_ref)

    t = pl.program_id(1)
    tl, ch = u_ref.shape[1], u_ref.shape[2]
    buf[pl.ds(CONV_HALO, tl), :] = u_ref[0]
    buf[pl.ds(0, CONV_HALO), :] = jnp.where(t > 0, ul_ref[0], jnp.zeros_like(ul_ref[0]))
    buf[pl.ds(CONV_HALO + tl, CONV_HALO), :] = jnp.where(t < pl.num_programs(1) - 1, ur_ref[0],
                                                          jnp.zeros_like(ur_ref[0]))

    shift0 = CONV_HALO - CONV_K // 2
    win = rb + 2 * CONV_HALO
    span = win - SUBLANES
    r_i = lax.broadcasted_iota(jnp.int32, (span, win), 0)
    c_i = lax.broadcasted_iota(jnp.int32, (span, win), 1)
    shifts = [(c_i == r_i + p).astype(BF16) for p in range(SUBLANES)]

    def row_block(i, carry):
        r0 = pl.multiple_of(i * rb, rb)

        for l0 in range(0, ch, CONV_SLAB):
            ls = slice(l0, l0 + CONV_SLAB)
            w = buf[pl.ds(r0, win), ls]
            acc = jnp.zeros((rb, CONV_SLAB), F32)
            for p in range(SUBLANES):
                sp = _dot(shifts[p], w)
                for k in range(CONV_K):
                    if (k + shift0) % SUBLANES == p:
                        a0 = (k + shift0) // SUBLANES * SUBLANES
                        acc = acc + sp[a0:a0 + rb, :] * cw_ref[k:k + 1, ls]
            y_scr[pl.ds(r0, rb), ls] = acc
        return carry

    lax.fori_loop(0, tl // rb, row_block, 0, unroll=True)

    y = y_scr[...] + cb_ref[...]
    mu = jnp.mean(y, axis=-1, keepdims=True)
    yc = y - mu
    var = jnp.mean(yc * yc, axis=-1, keepdims=True)
    yn = yc * lax.rsqrt(var + NORM_EPS) * lnw_ref[...] + lnb_ref[...]
    o_ref[0] = (yn * _sigmoid(yn)).astype(o_ref.dtype)


def _conv(z3, conv_w, conv_b, ln_w, ln_b, tl, rb, cast_weights, cc, w_ada, b_ada, ada_col0):
    _, bsz, seq, _ = z3.shape
    ch = conv_w.shape[1]
    hb = tl // CONV_HALO
    n_halo = seq // CONV_HALO
    nt = seq // tl
    n_steps = bsz * nt
    vec = pl.BlockSpec((1, ch), lambda b, t: (0, 0))
    for w in cast_weights:
        assert w.shape[0] % (n_steps * 16) == 0, w.shape
    cast_specs = [pl.BlockSpec((w.shape[0] // n_steps, w.shape[1]), lambda b, t: (b * nt + t, 0))
                  for w in cast_weights]
    n_ada = w_ada.shape[1] - ada_col0
    ta = n_ada // n_steps
    assert n_ada % n_steps == 0 and ta % 128 == 0 and ada_col0 % ta == 0
    ada_specs = [pl.BlockSpec(cc.shape, lambda b, t: (0, 0)),
                 pl.BlockSpec((w_ada.shape[0], ta), lambda b, t: (0, ada_col0 // ta + b * nt + t)),
                 pl.BlockSpec((1, ta), lambda b, t: (0, ada_col0 // ta + b * nt + t))]
    outs = pl.pallas_call(
        functools.partial(_conv_kernel, rb=rb, n_cast=len(cast_weights)),
        grid=(bsz, nt),
        in_specs=[pl.BlockSpec((None, 1, tl, ch), lambda b, t: (0, b, t, 0)),
                  pl.BlockSpec((None, 1, CONV_HALO, ch), lambda b, t: (0, b, jnp.maximum(t * hb - 1, 0), 0)),
                  pl.BlockSpec((None, 1, CONV_HALO, ch),
                               lambda b, t: (0, b, jnp.minimum((t + 1) * hb, n_halo - 1), 0)),
                  pl.BlockSpec((CONV_K, ch), lambda b, t: (0, 0)),
                  vec, vec, vec] + ada_specs + cast_specs,
        out_specs=[pl.BlockSpec((1, tl, ch), lambda b, t: (b, t, 0)),
                   pl.BlockSpec((cc.shape[0], ta), lambda b, t: (0, b * nt + t))] + cast_specs,
        out_shape=[jax.ShapeDtypeStruct((bsz, seq, ch), BF16), jax.ShapeDtypeStruct((cc.shape[0], n_ada), F32)]
                  + [jax.ShapeDtypeStruct(w.shape, BF16) for w in cast_weights],
        scratch_shapes=[pltpu.VMEM((tl + 2 * CONV_HALO, ch), BF16), pltpu.VMEM((tl, ch), F32)],
        compiler_params=_params(2),
        name="conv",
    )(z3, z3, z3, conv_w, conv_b, ln_w, ln_b, cc, w_ada, b_ada, *cast_weights)
    return outs[0], outs[1], outs[2:]


class _Scan:
    def __init__(self, q_ref, v_ref, g_ref, o_ref, st, scr, backward):
        self.q_ref, self.v_ref, self.g_ref, self.o_ref, self.st = q_ref, v_ref, g_ref, o_ref, st
        self.kd_s, self.qa_s, self.ka_s, self.qe_s, self.dec_s, self.kv_s, self.sb_s = scr
        rows, width = q_ref.shape[1], q_ref.shape[2]
        self.n_chunks, self.n_heads = rows // REC_CHUNK, width // REC_DK
        r_i = lax.broadcasted_iota(jnp.int32, (rows, rows), 0)
        c_i = lax.broadcasted_iota(jnp.int32, (rows, rows), 1)
        same_chunk = (r_i // REC_CHUNK) == (c_i // REC_CHUNK)
        tri = (same_chunk & ((c_i >= r_i) if backward else (c_i <= r_i))).astype(BF16)
        m_r = lax.broadcasted_iota(jnp.int32, (REC_CHUNK, REC_CHUNK), 0)
        m_c = lax.broadcasted_iota(jnp.int32, (REC_CHUNK, REC_CHUNK), 1)
        self.mask = (m_c >= m_r) if backward else (m_c <= m_r)
        ref_f = REC_CHUNK // 2 - 1
        self.last, self.ref = (0, REC_CHUNK - 1 - ref_f) if backward else (REC_CHUNK - 1, ref_f)
        self.order = list(range(self.n_chunks - 1, -1, -1) if backward else range(self.n_chunks))
        self.b = _tri_sum(tri, g_ref[0])
        self.scores = {}

    def _rows(self, c):
        return slice(c * REC_CHUNK, (c + 1) * REC_CHUNK)

    def _heads(self):
        return [(h, slice(h * REC_DK, (h + 1) * REC_DK)) for h in range(self.n_heads)]

    def prep(self, c):
        rs = self._rows(c)
        bc, qc = self.b[rs], self.q_ref[0, rs, :].astype(F32)
        b_last = bc[self.last:self.last + 1, :]
        b_ref = bc[self.ref:self.ref + 1, :]
        k = 1.0 - jnp.exp2(self.g_ref[0, rs, :].astype(F32))
        ka = k * jnp.exp2(b_ref - bc)
        qa = qc * jnp.exp2(bc - b_ref)
        self.ka_s[rs, :] = ka.astype(BF16)
        self.qa_s[rs, :] = qa.astype(BF16)
        self.kd_s[rs, :] = (ka * jnp.exp2(b_last - b_ref)).astype(BF16)
        self.qe_s[rs, :] = (qa * jnp.exp2(b_ref)).astype(BF16)
        self.dec_s[c:c + 1, :] = jnp.exp2(b_last)

    def intra(self, c):
        rs = self._rows(c)
        for h, ls in self._heads():
            s = _dot_nt(self.qa_s[rs, ls], self.ka_s[rs, ls])
            self.scores[c, h] = jnp.where(self.mask, s, 0.0).astype(BF16)
            self.kv_s[c, h] = _dot_tn(self.v_ref[0, rs, ls], self.kd_s[rs, ls])

    def recur(self, c):
        for h, ls in self._heads():
            s = self.st[h]
            self.sb_s[c, h] = s.astype(BF16)
            self.st[h] = self.dec_s[c:c + 1, ls] * s + self.kv_s[c, h]

    def out(self, c):
        rs = self._rows(c)
        for h, ls in self._heads():
            self.o_ref[0, rs, ls] = (_dot(self.scores[c, h], self.v_ref[0, rs, ls])
                                     + _dot_nt(self.qe_s[rs, ls], self.sb_s[c, h]))


def _gla_kernel(qf_ref, vf_ref, gf_ref, qb_ref, vb_ref, gb_ref, s0f_ref, s0b_ref,
                of_ref, ob_ref, stf, stb, *scr):
    @pl.when(pl.program_id(1) == 0)
    def _():
        stf[...] = s0f_ref[0]
        stb[...] = s0b_ref[0]

    half = len(scr) // 2
    fwd = _Scan(qf_ref, vf_ref, gf_ref, of_ref, stf, scr[:half], backward=False)
    bwd = _Scan(qb_ref, vb_ref, gb_ref, ob_ref, stb, scr[half:], backward=True)
    for cf, cb in zip(fwd.order, bwd.order):
        for scan, c in ((fwd, cf), (bwd, cb)):
            scan.prep(c)
        for scan, c in ((fwd, cf), (bwd, cb)):
            scan.intra(c)
        for scan, c in ((fwd, cf), (bwd, cb)):
            scan.recur(c)
            scan.out(c)


def _gla(z3, s0f, s0b, rows, width):
    _, bsz, seq, _ = z3.shape
    nb = seq // rows
    nh = width // REC_DK
    nc = rows // REC_CHUNK
    state = pltpu.VMEM((nh, REC_DK, REC_DK), F32)
    per_direction = ([pltpu.VMEM((rows, width), BF16)] * 4
                     + [pltpu.VMEM((max(nc, SUBLANES), width), F32),
                        pltpu.VMEM((nc, nh, REC_DK, REC_DK), F32),
                        pltpu.VMEM((nc, nh, REC_DK, REC_DK), BF16)])
    fwd = lambda col: pl.BlockSpec((None, 1, rows, width), lambda b, n, col=col: (col, b, n, 0))
    bwd = lambda col: pl.BlockSpec((None, 1, rows, width), lambda b, n, col=col: (col, b, nb - 1 - n, 0))
    st_spec = pl.BlockSpec((1, nh, REC_DK, REC_DK), lambda b, n: (b, 0, 0, 0))
    o_shape = jax.ShapeDtypeStruct((bsz, seq, width), F32)
    return pl.pallas_call(
        _gla_kernel,
        grid=(bsz, nb),
        in_specs=[fwd(1), fwd(3), fwd(4), bwd(1), bwd(3), bwd(5), st_spec, st_spec],
        out_specs=[pl.BlockSpec((1, rows, width), lambda b, n: (b, n, 0)),
                   pl.BlockSpec((1, rows, width), lambda b, n: (b, nb - 1 - n, 0))],
        out_shape=[o_shape, o_shape],
        scratch_shapes=[state, state] + per_direction * 2,
        compiler_params=_params(2),
        name="gla",
    )(z3, z3, z3, z3, z3, z3, s0f, s0b)


def _outproj_kernel(u_ref, of_ref, ob_ref, gs_ref, x_ref, mod_ref, rnw_ref, postw_ref, w_ref, x1_ref):
    gw = u_ref.shape[1]
    o = of_ref[...] + ob_ref[...]
    heads = []
    for h in range(gw // REC_DK):
        oh = o[:, h * REC_DK:(h + 1) * REC_DK]
        heads.append(oh * lax.rsqrt(jnp.mean(oh * oh, axis=-1, keepdims=True) + NORM_EPS))
    on = jnp.concatenate(heads, axis=-1) * rnw_ref[...] * gs_ref[...]
    y = _dot(u_ref[...].astype(BF16), w_ref[0:gw, :]) + _dot(on.astype(BF16), w_ref[gw:2 * gw, :])
    x1_ref[...] = x_ref[...] + mod_ref[0, 0:1, :] * _rms_norm(y, postw_ref[...])


def _outproj(uconv, o_f, o_b, z2, x2, mod3, rnw, postw, w_out_bf, tm, rows_per_batch):
    m, d = x2.shape
    gw = d // 2
    tiles_per_batch = rows_per_batch // tm
    half = pl.BlockSpec((tm, gw), lambda i: (i, 0))
    full = pl.BlockSpec((tm, d), lambda i: (i, 0))
    vec = lambda n: pl.BlockSpec((1, n), lambda i: (0, 0))
    return pl.pallas_call(
        _outproj_kernel,
        grid=(m // tm,),
        in_specs=[half, half, half,
                  pl.BlockSpec((None, tm, gw), lambda i: (2, i, 0)),
                  full,
                  pl.BlockSpec((1,) + mod3.shape[1:], lambda i: (i // tiles_per_batch, 0, 0)),
                  vec(gw), vec(d),
                  pl.BlockSpec((d, d), lambda i: (0, 0))],
        out_specs=full,
        out_shape=jax.ShapeDtypeStruct((m, d), F32),
        compiler_params=_params(1),
        name="outproj",
    )(uconv, o_f, o_b, z2, x2, mod3, rnw, postw, w_out_bf)


def _mlp_kernel(x1_ref, mod_ref, prew_ref, postw_ref, wu_ref, wd_ref, o_ref, h_scr):
    k = pl.program_id(1)
    last = pl.num_programs(1) - 1

    def partial_sum(rs):
        hid = jnp.maximum(_dot(h_scr[rs, :], wu_ref[...]), 0.0)
        return _dot((hid * hid).astype(BF16), wd_ref[...])

    @pl.when(k == 0)
    def _():
        for rs in _row_tiles(h_scr.shape[0], MLP_SUB_M):
            h = _rms_norm(x1_ref[rs, :], prew_ref[...])
            h_scr[rs, :] = (h * (1.0 + mod_ref[0, 2:3, :]) + mod_ref[0, 1:2, :]).astype(BF16)
            o_ref[rs, :] = partial_sum(rs)

    @pl.when((k > 0) & (k < last))
    def _():
        for rs in _row_tiles(h_scr.shape[0], MLP_SUB_M):
            o_ref[rs, :] += partial_sum(rs)

    @pl.when(k == last)
    def _():
        for rs in _row_tiles(h_scr.shape[0], MLP_SUB_M):
            y = o_ref[rs, :] + partial_sum(rs)
            o_ref[rs, :] = x1_ref[rs, :] + mod_ref[0, 3:4, :] * _rms_norm(y, postw_ref[...])


def _mlp(x1, mod3, prew, postw, w_up_bf, w_down_bf, tm, tf, rows_per_batch):
    m, d = x1.shape
    ff = w_up_bf.shape[1]
    assert ff // tf >= 2, "the first and last hidden chunks take different branches"
    tiles_per_batch = rows_per_batch // tm
    vec = pl.BlockSpec((1, d), lambda i, k: (0, 0))
    sub = min(tm, MLP_SUB_M)
    vmem_estimate = (2 * 2 * tm * d * 4 + tm * d * 2 + 2 * 2 * d * tf * 2
                     + sub * (tf * 4 + tf * 2 + d * 4) + VMEM_COMPILER_ALLOWANCE)
    return pl.pallas_call(
        _mlp_kernel,
        grid=(m // tm, ff // tf),
        in_specs=[pl.BlockSpec((tm, d), lambda i, k: (i, 0)),
                  pl.BlockSpec((1,) + mod3.shape[1:], lambda i, k: (i // tiles_per_batch, 0, 0)),
                  vec, vec,
                  pl.BlockSpec((d, tf), lambda i, k: (0, k)),
                  pl.BlockSpec((tf, d), lambda i, k: (k, 0))],
        out_specs=pl.BlockSpec((tm, d), lambda i, k: (i, 0)),
        out_shape=jax.ShapeDtypeStruct((m, d), F32),
        scratch_shapes=[pltpu.VMEM((tm, d), BF16)],
        compiler_params=_params(2, max(vmem_estimate, VMEM_LIMIT_BYTES)),
        name="mlp",
    )(x1, mod3, prew, postw, w_up_bf, w_down_bf)


def _tile(n, target):
    t = min(n, target)
    assert n % t == 0, (n, t)
    return t


def kernel(x, c, ctx, c_ctx, w_ada, b_ada, mix_pre_w, mix_post_w, mlp_pre_w, mlp_post_w, w_in, conv_w, conv_b,
           conv_ln_w, conv_ln_b, rec_lb_logits, rec_norm_w, w_out, w_up, w_down):
    bsz, seq, d = x.shape
    assert w_in.shape[0] == 1, "single-layer block"
    gw = d // 2
    assert gw % REC_DK == 0 and seq % REC_CHUNK == 0 and ctx.shape[1] % SUBLANES == 0
    m = bsz * seq

    cc = jnp.concatenate([c, c_ctx[None, :], jnp.zeros((SUBLANES - bsz - 1, d), c.dtype)], axis=0)
    mod_a = _ada(cc, w_ada[0], b_ada, _tile(2 * d, 1024), 2 * d).reshape(SUBLANES, 2, d)

    lbl = rec_lb_logits.astype(F32)

    s0f, s0b = _ctx_states(ctx, mod_a, mix_pre_w, lbl, w_in[0], col0=2 * gw + 2 * gw,
                           heads_per_step=min(4, gw // REC_DK))

    x2 = x.reshape(m, d)
    tm = _tile(seq, 512)
    z2 = _inproj(x2, mod_a, mix_pre_w, lbl, w_in[0], _tile(seq, 1024), _tile(gw, 512), seq)
    z3 = z2.reshape(z2.shape[0], bsz, seq, gw)

    tl = _tile(seq, 512)
    uconv, mod_b, (w_out_bf, w_up_bf, w_down_bf) = _conv(
        z3, conv_w[0], conv_b, conv_ln_w, conv_ln_b, tl, _tile(tl, 64), [w_out[0], w_up[0], w_down[0]],
        cc, w_ada[0], b_ada, ada_col0=2 * d)
    mod_b = mod_b.reshape(SUBLANES, 4, d)
    o_f, o_b = _gla(z3, s0f, s0b, _tile(seq, 512), gw)

    x1 = _outproj(uconv.reshape(m, gw), o_f.reshape(m, gw), o_b.reshape(m, gw), z2, x2, mod_b,
                  rec_norm_w, mix_post_w, w_out_bf, _tile(seq, 512), seq)
    out = _mlp(x1, mod_b, mlp_pre_w, mlp_post_w, w_up_bf, w_down_bf, _tile(seq, 1024), _tile(w_up.shape[2], 1024), seq)
    return out.reshape(bsz, seq, d)
```

```python
import functools

import jax
import jax.numpy as jnp
from jax import lax
from jax.experimental import pallas as pl
from jax.experimental.pallas import tpu as pltpu

NORM_EPS = 1e-6
REC_DK = 128
REC_CHUNK = 64
TRI_ROWS = 256
CONV_K = 31
CONV_HALO = 16
CONV_SLAB = 256
SUBLANES = 8
PROJ_SUB_M = 512
MLP_SUB_M = 512
V7X_VMEM_BYTES = 64 * 1024 * 1024
VMEM_LIMIT_BYTES = 56 * 1024 * 1024
VMEM_COMPILER_ALLOWANCE = 3 * 1024 * 1024

F32 = jnp.float32
BF16 = jnp.bfloat16


def _params(n_axes, vmem_limit_bytes=VMEM_LIMIT_BYTES):
    assert vmem_limit_bytes < V7X_VMEM_BYTES
    return pltpu.CompilerParams(dimension_semantics=("arbitrary",) * n_axes,
                                vmem_limit_bytes=vmem_limit_bytes)


def _row_tiles(rows, sub):
    sub = min(rows, sub)
    assert rows % sub == 0, (rows, sub)
    return [slice(r0, r0 + sub) for r0 in range(0, rows, sub)]


def _sigmoid(x):
    return 1.0 / (1.0 + jnp.exp(-x))


def _rms_norm(xf, w):
    ms = jnp.mean(xf * xf, axis=-1, keepdims=True)
    return xf * lax.rsqrt(ms + NORM_EPS) * w


def _dot(a, b):
    return jnp.dot(a, b, preferred_element_type=F32)


def _dot_nt(a, b):
    return lax.dot_general(a, b, (((1,), (1,)), ((), ())), preferred_element_type=F32)


def _dot_tn(a, b):
    return lax.dot_general(a, b, (((0,), (0,)), ((), ())), preferred_element_type=F32)


def _tri_sum(tri, g):
    if g.dtype == BF16:
        return _dot(tri, g)
    g_hi = g.astype(BF16)
    g_lo = (g - g_hi.astype(F32)).astype(BF16)
    return _dot(tri, g_hi) + _dot(tri, g_lo)


def _lower_bound(lbl, direction):
    m = jnp.max(lbl, axis=0)
    e = jnp.exp(lbl - m[None])
    sm0 = e[0] / jnp.sum(e, axis=0)
    return sm0[direction:direction + 1]


def _log_forget(z, lb):
    return jnp.log2(lb + (1.0 - lb) * _sigmoid(z))


def _ada_kernel(c_ref, w_ref, b_ref, o_ref):
    c = c_ref[...]
    s = (c * _sigmoid(c)).astype(BF16)
    o_ref[...] = _dot(s, w_ref[...].astype(BF16)) + b_ref[...]


def _ada(cc, w_ada, b_ada, tn, n_cols):
    rows, d = cc.shape
    return pl.pallas_call(
        _ada_kernel,
        grid=(n_cols // tn,),
        in_specs=[pl.BlockSpec((rows, d), lambda j: (0, 0)),
                  pl.BlockSpec((d, tn), lambda j: (0, j)),
                  pl.BlockSpec((1, tn), lambda j: (0, j))],
        out_specs=pl.BlockSpec((rows, tn), lambda j: (0, j)),
        out_shape=jax.ShapeDtypeStruct((rows, n_cols), F32),
        compiler_params=_params(1),
        name="ada",
    )(cc, w_ada, b_ada)


def _ctx_kernel(ctx_ref, mod_ref, npw_ref, lbl_ref, wi_ref, wf_ref, wb_ref, sf_ref, sb_ref, h_scr):
    @pl.when(pl.program_id(1) == 0)
    def _():
        h = _rms_norm(ctx_ref[0], npw_ref[...])
        h_scr[...] = (h * (1.0 + mod_ref[0, 1:2, :]) + mod_ref[0, 0:1, :]).astype(BF16)

    hb = h_scr[...]
    lc = hb.shape[0]
    v = _dot(hb, wi_ref[...].astype(BF16)).astype(BF16)
    lbl = lbl_ref[...]
    g_f = _log_forget(_dot(hb, wf_ref[...].astype(BF16)), _lower_bound(lbl, 0))
    g_b = _log_forget(_dot(hb, wb_ref[...].astype(BF16)), _lower_bound(lbl, 1))
    row = lax.broadcasted_iota(jnp.int32, (lc, lc), 0)
    col = lax.broadcasted_iota(jnp.int32, (lc, lc), 1)
    incl = (col <= row).astype(BF16)
    excl = (col < row).astype(BF16)
    b_f = _tri_sum(incl, g_f)
    kd_f = ((1.0 - jnp.exp2(g_f)) * jnp.exp2(b_f[lc - 1:lc, :] - b_f)).astype(BF16)
    c_b = _tri_sum(excl, g_b)
    kd_b = ((1.0 - jnp.exp2(g_b)) * jnp.exp2(c_b)).astype(BF16)
    for h in range(v.shape[1] // REC_DK):
        ls = slice(h * REC_DK, (h + 1) * REC_DK)
        sf_ref[0, h] = _dot_tn(v[:, ls], kd_f[:, ls])
        sb_ref[0, h] = _dot_tn(v[:, ls], kd_b[:, ls])


def _ctx_states(ctx, mod3, npw, lbl, w_in, col0, heads_per_step):
    bsz, lc, d = ctx.shape
    nh = lbl.shape[-1] // REC_DK
    hg = heads_per_step
    wcols = hg * REC_DK
    cb = col0 // wcols
    ng = nh // hg
    wspec = lambda off: pl.BlockSpec((d, wcols), lambda b, g, off=off: (0, cb + off * ng + g))
    st_spec = pl.BlockSpec((1, hg, REC_DK, REC_DK), lambda b, g: (b, g, 0, 0))
    st_shape = jax.ShapeDtypeStruct((bsz, nh, REC_DK, REC_DK), F32)
    return pl.pallas_call(
        _ctx_kernel,
        grid=(bsz, ng),
        in_specs=[pl.BlockSpec((1, lc, d), lambda b, g: (b, 0, 0)),
                  pl.BlockSpec((1,) + mod3.shape[1:], lambda b, g: (bsz, 0, 0)),
                  pl.BlockSpec((1, d), lambda b, g: (0, 0)),
                  pl.BlockSpec(lbl.shape[:2] + (wcols,), lambda b, g: (0, 0, g)),
                  wspec(0), wspec(1), wspec(2)],
        out_specs=[st_spec, st_spec],
        out_shape=[st_shape, st_shape],
        scratch_shapes=[pltpu.VMEM((lc, d), BF16)],
        compiler_params=_params(2),
        name="ctx_states",
    )(ctx, mod3, npw, lbl, w_in, w_in, w_in)


def _project_tiles(h_scr, w_ref, wb_scr, tn, epilogue):
    for rs in _row_tiles(h_scr.shape[0], PROJ_SUB_M):
        for n0 in range(0, w_ref.shape[1], tn):
            cs = slice(n0, n0 + tn)
            if rs.start == 0:
                wb_scr[:, cs] = w_ref[:, cs].astype(BF16)
            epilogue(rs, cs, _dot(h_scr[rs, :], wb_scr[:, cs]))


def _inproj_kernel(x_ref, mod_ref, npw_ref, lbl_ref, w_ref, o_ref, h_scr, a_scr, wb_scr, *, tn):
    j = pl.program_id(1)

    @pl.when(j == 0)
    def _():
        h = _rms_norm(x_ref[...], npw_ref[...])
        h_scr[...] = (h * (1.0 + mod_ref[0, 1:2, :]) + mod_ref[0, 0:1, :]).astype(BF16)

        def park(rs, cs, z):
            a_scr[rs, cs] = z
        _project_tiles(h_scr, w_ref, wb_scr, tn, park)

    @pl.when(j == 1)
    def _():
        def glu(rs, cs, z):
            o_ref[rs, cs] = (a_scr[rs, cs] * _sigmoid(z)).astype(o_ref.dtype)
        _project_tiles(h_scr, w_ref, wb_scr, tn, glu)

    @pl.when((j == 2) | (j == 3))
    def _():
        def swish(rs, cs, z):
            o_ref[rs, cs] = (z * _sigmoid(z)).astype(o_ref.dtype)
        _project_tiles(h_scr, w_ref, wb_scr, tn, swish)

    @pl.when(j == 4)
    def _():
        def values(rs, cs, z):
            o_ref[rs, cs] = z.astype(o_ref.dtype)
        _project_tiles(h_scr, w_ref, wb_scr, tn, values)

    @pl.when(j >= 5)
    def _():
        lbl = lbl_ref[...]
        lb = jnp.where(j == 5, _lower_bound(lbl, 0), _lower_bound(lbl, 1))

        def log_gate(rs, cs, z):
            o_ref[rs, cs] = _log_forget(z, lb[:, cs]).astype(o_ref.dtype)
        _project_tiles(h_scr, w_ref, wb_scr, tn, log_gate)


def _inproj(x2, mod3, npw, lbl, w_in, tm, tn, rows_per_batch):
    m, d = x2.shape
    gw = d // 2
    n_groups = w_in.shape[1] // gw
    tiles_per_batch = rows_per_batch // tm
    return pl.pallas_call(
        functools.partial(_inproj_kernel, tn=tn),
        grid=(m // tm, n_groups),
        in_specs=[pl.BlockSpec((tm, d), lambda i, j: (i, 0)),
                  pl.BlockSpec((1,) + mod3.shape[1:], lambda i, j: (i // tiles_per_batch, 0, 0)),
                  pl.BlockSpec((1, d), lambda i, j: (0, 0)),
                  pl.BlockSpec(lbl.shape, lambda i, j: (0, 0, 0)),
                  pl.BlockSpec((d, gw), lambda i, j: (0, j))],
        out_specs=pl.BlockSpec((None, tm, gw), lambda i, j: (jnp.maximum(j - 1, 0), i, 0)),
        out_shape=jax.ShapeDtypeStruct((n_groups - 1, m, gw), BF16),
        scratch_shapes=[pltpu.VMEM((tm, d), BF16), pltpu.VMEM((tm, gw), F32), pltpu.VMEM((d, gw), BF16)],
        compiler_params=_params(2),
        name="inproj",
    )(x2, mod3, npw, lbl, w_in)


def _conv_kernel(u_ref, ul_ref, ur_ref, cw_ref, cb_ref, lnw_ref, lnb_ref, cc_ref, wada_ref, bada_ref, *rest,
                 rb, n_cast):
    cast_in, o_ref, modb_ref = rest[:n_cast], rest[n_cast], rest[n_cast + 1]
    cast_out = rest[n_cast + 2:2 * n_cast + 2]
    buf, y_scr = rest[2 * n_cast + 2:]
    for src, dst in zip(cast_in, cast_out):
        dst[...] = src[...].astype(dst.dtype)
    _ada_kernel(cc_ref, wada_ref, bada_ref, modb_ref)

    t = pl.program_id(1)
    tl, ch = u_ref.shape[1], u_ref.shape[2]
    buf[pl.ds(CONV_HALO, tl), :] = u_ref[0]
    buf[pl.ds(0, CONV_HALO), :] = jnp.where(t > 0, ul_ref[0], jnp.zeros_like(ul_ref[0]))
    buf[pl.ds(CONV_HALO + tl, CONV_HALO), :] = jnp.where(t < pl.num_programs(1) - 1, ur_ref[0],
                                                          jnp.zeros_like(ur_ref[0]))

    shift0 = CONV_HALO - CONV_K // 2
    win = rb + 2 * CONV_HALO
    span = win - SUBLANES
    r_i = lax.broadcasted_iota(jnp.int32, (SUBLANES * span, win), 0)
    c_i = lax.broadcasted_iota(jnp.int32, (SUBLANES * span, win), 1)
    shifts = (c_i == r_i % span + r_i // span).astype(BF16)

    def row_block(i, carry):
        r0 = pl.multiple_of(i * rb, rb)

        for l0 in range(0, ch, CONV_SLAB):
            ls = slice(l0, l0 + CONV_SLAB)
            shifted = _dot(shifts, buf[pl.ds(r0, win), ls])
            acc = jnp.zeros((rb, CONV_SLAB), F32)
            for p in range(SUBLANES):
                sp = shifted[p * span:(p + 1) * span, :]
                for k in range(CONV_K):
                    if (k + shift0) % SUBLANES == p:
                        a0 = (k + shift0) // SUBLANES * SUBLANES
                        acc = acc + sp[a0:a0 + rb, :] * cw_ref[k:k + 1, ls]
            y_scr[pl.ds(r0, rb), ls] = acc
        return carry

    lax.fori_loop(0, tl // rb, row_block, 0, unroll=True)

    y = y_scr[...] + cb_ref[...]
    mu = jnp.mean(y, axis=-1, keepdims=True)
    yc = y - mu
    var = jnp.mean(yc * yc, axis=-1, keepdims=True)
    yn = yc * lax.rsqrt(var + NORM_EPS) * lnw_ref[...] + lnb_ref[...]
    o_ref[0] = (yn * _sigmoid(yn)).astype(o_ref.dtype)


def _conv(z3, conv_w, conv_b, ln_w, ln_b, tl, rb, cast_weights, cc, w_ada, b_ada, ada_col0):
    _, bsz, seq, _ = z3.shape
    ch = conv_w.shape[1]
    hb = tl // CONV_HALO
    n_halo = seq // CONV_HALO
    nt = seq // tl
    n_steps = bsz * nt
    vec = pl.BlockSpec((1, ch), lambda b, t: (0, 0))
    for w in cast_weights:
        assert w.shape[0] % (n_steps * 16) == 0, w.shape
    cast_specs = [pl.BlockSpec((w.shape[0] // n_steps, w.shape[1]), lambda b, t: (b * nt + t, 0))
                  for w in cast_weights]
    n_ada = w_ada.shape[1] - ada_col0
    ta = n_ada // n_steps
    assert n_ada % n_steps == 0 and ta % 128 == 0 and ada_col0 % ta == 0
    ada_specs = [pl.BlockSpec(cc.shape, lambda b, t: (0, 0)),
                 pl.BlockSpec((w_ada.shape[0], ta), lambda b, t: (0, ada_col0 // ta + b * nt + t)),
                 pl.BlockSpec((1, ta), lambda b, t: (0, ada_col0 // ta + b * nt + t))]
    outs = pl.pallas_call(
        functools.partial(_conv_kernel, rb=rb, n_cast=len(cast_weights)),
        grid=(bsz, nt),
        in_specs=[pl.BlockSpec((None, 1, tl, ch), lambda b, t: (0, b, t, 0)),
                  pl.BlockSpec((None, 1, CONV_HALO, ch), lambda b, t: (0, b, jnp.maximum(t * hb - 1, 0), 0)),
                  pl.BlockSpec((None, 1, CONV_HALO, ch),
                               lambda b, t: (0, b, jnp.minimum((t + 1) * hb, n_halo - 1), 0)),
                  pl.BlockSpec((CONV_K, ch), lambda b, t: (0, 0)),
                  vec, vec, vec] + ada_specs + cast_specs,
        out_specs=[pl.BlockSpec((1, tl, ch), lambda b, t: (b, t, 0)),
                   pl.BlockSpec((cc.shape[0], ta), lambda b, t: (0, b * nt + t))] + cast_specs,
        out_shape=[jax.ShapeDtypeStruct((bsz, seq, ch), BF16), jax.ShapeDtypeStruct((cc.shape[0], n_ada), F32)]
                  + [jax.ShapeDtypeStruct(w.shape, BF16) for w in cast_weights],
        scratch_shapes=[pltpu.VMEM((tl + 2 * CONV_HALO, ch), BF16), pltpu.VMEM((tl, ch), F32)],
        compiler_params=_params(2),
        name="conv",
    )(z3, z3, z3, conv_w, conv_b, ln_w, ln_b, cc, w_ada, b_ada, *cast_weights)
    return outs[0], outs[1], outs[2:]


class _Scan:
    def __init__(self, q_ref, v_ref, g_ref, o_ref, st, scr, backward):
        self.q_ref, self.v_ref, self.g_ref, self.o_ref, self.st = q_ref, v_ref, g_ref, o_ref, st
        self.kd_s, self.qa_s, self.ka_s, self.qe_s, self.dec_s, self.kv_s, self.sb_s = scr
        rows, width = q_ref.shape[1], q_ref.shape[2]
        self.n_chunks, self.n_heads = rows // REC_CHUNK, width // REC_DK
        span = min(rows, TRI_ROWS)
        r_i = lax.broadcasted_iota(jnp.int32, (span, span), 0)
        c_i = lax.broadcasted_iota(jnp.int32, (span, span), 1)
        same_chunk = (r_i // REC_CHUNK) == (c_i // REC_CHUNK)
        tri = (same_chunk & ((c_i >= r_i) if backward else (c_i <= r_i))).astype(BF16)
        m_r = lax.broadcasted_iota(jnp.int32, (REC_CHUNK, REC_CHUNK), 0)
        m_c = lax.broadcasted_iota(jnp.int32, (REC_CHUNK, REC_CHUNK), 1)
        self.mask = (m_c >= m_r) if backward else (m_c <= m_r)
        ref_f = REC_CHUNK // 2 - 1
        self.last, self.ref = (0, REC_CHUNK - 1 - ref_f) if backward else (REC_CHUNK - 1, ref_f)
        self.order = list(range(self.n_chunks - 1, -1, -1) if backward else range(self.n_chunks))
        self.b = jnp.concatenate([_tri_sum(tri, g_ref[0, rs, :]) for rs in _row_tiles(rows, span)], axis=0)
        self.scores = {}

    def _rows(self, c):
        return slice(c * REC_CHUNK, (c + 1) * REC_CHUNK)

    def _heads(self):
        return [(h, slice(h * REC_DK, (h + 1) * REC_DK)) for h in range(self.n_heads)]

    def prep(self, c):
        rs = self._rows(c)
        bc, qc = self.b[rs], self.q_ref[0, rs, :].astype(F32)
        b_last = bc[self.last:self.last + 1, :]
        b_ref = bc[self.ref:self.ref + 1, :]
        k = 1.0 - jnp.exp2(self.g_ref[0, rs, :].astype(F32))
        ka = k * jnp.exp2(b_ref - bc)
        qa = qc * jnp.exp2(bc - b_ref)
        self.ka_s[rs, :] = ka.astype(BF16)
        self.qa_s[rs, :] = qa.astype(BF16)
        self.kd_s[rs, :] = (ka * jnp.exp2(b_last - b_ref)).astype(BF16)
        self.qe_s[rs, :] = (qa * jnp.exp2(b_ref)).astype(BF16)
        self.dec_s[c:c + 1, :] = jnp.exp2(b_last)

    def intra(self, c):
        rs = self._rows(c)
        for h, ls in self._heads():
            s = _dot_nt(self.qa_s[rs, ls], self.ka_s[rs, ls])
            self.scores[c, h] = jnp.where(self.mask, s, 0.0).astype(BF16)
            self.kv_s[c, h] = _dot_tn(self.v_ref[0, rs, ls], self.kd_s[rs, ls])

    def recur(self, c):
        for h, ls in self._heads():
            s = self.st[h]
            self.sb_s[c, h] = s.astype(BF16)
            self.st[h] = self.dec_s[c:c + 1, ls] * s + self.kv_s[c, h]

    def out(self, c):
        rs = self._rows(c)
        for h, ls in self._heads():
            self.o_ref[0, rs, ls] = (_dot(self.scores[c, h], self.v_ref[0, rs, ls])
                                     + _dot_nt(self.qe_s[rs, ls], self.sb_s[c, h]))


def _gla_kernel(qf_ref, vf_ref, gf_ref, qb_ref, vb_ref, gb_ref, s0f_ref, s0b_ref,
                of_ref, ob_ref, stf, stb, *scr):
    @pl.when(pl.program_id(1) == 0)
    def _():
        stf[...] = s0f_ref[0]
        stb[...] = s0b_ref[0]

    half = len(scr) // 2
    fwd = _Scan(qf_ref, vf_ref, gf_ref, of_ref, stf, scr[:half], backward=False)
    bwd = _Scan(qb_ref, vb_ref, gb_ref, ob_ref, stb, scr[half:], backward=True)
    for cf, cb in zip(fwd.order, bwd.order):
        for scan, c in ((fwd, cf), (bwd, cb)):
            scan.prep(c)
        for scan, c in ((fwd, cf), (bwd, cb)):
            scan.intra(c)
        for scan, c in ((fwd, cf), (bwd, cb)):
            scan.recur(c)
            scan.out(c)


def _gla(z3, s0f, s0b, rows, width):
    _, bsz, seq, _ = z3.shape
    nb = seq // rows
    nh = width // REC_DK
    nc = rows // REC_CHUNK
    state = pltpu.VMEM((nh, REC_DK, REC_DK), F32)
    per_direction = ([pltpu.VMEM((rows, width), BF16)] * 4
                     + [pltpu.VMEM((max(nc, SUBLANES), width), F32),
                        pltpu.VMEM((nc, nh, REC_DK, REC_DK), F32),
                        pltpu.VMEM((nc, nh, REC_DK, REC_DK), BF16)])
    fwd = lambda col: pl.BlockSpec((None, 1, rows, width), lambda b, n, col=col: (col, b, n, 0))
    bwd = lambda col: pl.BlockSpec((None, 1, rows, width), lambda b, n, col=col: (col, b, nb - 1 - n, 0))
    st_spec = pl.BlockSpec((1, nh, REC_DK, REC_DK), lambda b, n: (b, 0, 0, 0))
    o_shape = jax.ShapeDtypeStruct((bsz, seq, width), F32)
    return pl.pallas_call(
        _gla_kernel,
        grid=(bsz, nb),
        in_specs=[fwd(1), fwd(3), fwd(4), bwd(1), bwd(3), bwd(5), st_spec, st_spec],
        out_specs=[pl.BlockSpec((1, rows, width), lambda b, n: (b, n, 0)),
                   pl.BlockSpec((1, rows, width), lambda b, n: (b, nb - 1 - n, 0))],
        out_shape=[o_shape, o_shape],
        scratch_shapes=[state, state] + per_direction * 2,
        compiler_params=_params(2),
        name="gla",
    )(z3, z3, z3, z3, z3, z3, s0f, s0b)


def _outproj_kernel(u_ref, of_ref, ob_ref, gs_ref, x_ref, mod_ref, rnw_ref, postw_ref, w_ref, x1_ref):
    gw = u_ref.shape[1]
    o = of_ref[...] + ob_ref[...]
    heads = []
    for h in range(gw // REC_DK):
        oh = o[:, h * REC_DK:(h + 1) * REC_DK]
        heads.append(oh * lax.rsqrt(jnp.mean(oh * oh, axis=-1, keepdims=True) + NORM_EPS))
    on = jnp.concatenate(heads, axis=-1) * rnw_ref[...] * gs_ref[...]
    y = _dot(u_ref[...].astype(BF16), w_ref[0:gw, :]) + _dot(on.astype(BF16), w_ref[gw:2 * gw, :])
    x1_ref[...] = x_ref[...] + mod_ref[0, 0:1, :] * _rms_norm(y, postw_ref[...])


def _outproj(uconv, o_f, o_b, z2, x2, mod3, rnw, postw, w_out_bf, tm, rows_per_batch):
    m, d = x2.shape
    gw = d // 2
    tiles_per_batch = rows_per_batch // tm
    half = pl.BlockSpec((tm, gw), lambda i: (i, 0))
    full = pl.BlockSpec((tm, d), lambda i: (i, 0))
    vec = lambda n: pl.BlockSpec((1, n), lambda i: (0, 0))
    return pl.pallas_call(
        _outproj_kernel,
        grid=(m // tm,),
        in_specs=[half, half, half,
                  pl.BlockSpec((None, tm, gw), lambda i: (2, i, 0)),
                  full,
                  pl.BlockSpec((1,) + mod3.shape[1:], lambda i: (i // tiles_per_batch, 0, 0)),
                  vec(gw), vec(d),
                  pl.BlockSpec((d, d), lambda i: (0, 0))],
        out_specs=full,
        out_shape=jax.ShapeDtypeStruct((m, d), F32),
        compiler_params=_params(1),
        name="outproj",
    )(uconv, o_f, o_b, z2, x2, mod3, rnw, postw, w_out_bf)


def _mlp_kernel(x1_ref, mod_ref, prew_ref, postw_ref, wu_ref, wd_ref, o_ref, h_scr):
    k = pl.program_id(1)
    last = pl.num_programs(1) - 1

    def partial_sum(rs):
        hid = jnp.maximum(_dot(h_scr[rs, :], wu_ref[...]), 0.0)
        return _dot((hid * hid).astype(BF16), wd_ref[...])

    @pl.when(k == 0)
    def _():
        for rs in _row_tiles(h_scr.shape[0], MLP_SUB_M):
            h = _rms_norm(x1_ref[rs, :], prew_ref[...])
            h_scr[rs, :] = (h * (1.0 + mod_ref[0, 2:3, :]) + mod_ref[0, 1:2, :]).astype(BF16)
            o_ref[rs, :] = partial_sum(rs)

    @pl.when((k > 0) & (k < last))
    def _():
        for rs in _row_tiles(h_scr.shape[0], MLP_SUB_M):
            o_ref[rs, :] += partial_sum(rs)

    @pl.when(k == last)
    def _():
        for rs in _row_tiles(h_scr.shape[0], MLP_SUB_M):
            y = o_ref[rs, :] + partial_sum(rs)
            o_ref[rs, :] = x1_ref[rs, :] + mod_ref[0, 3:4, :] * _rms_norm(y, postw_ref[...])


def _mlp(x1, mod3, prew, postw, w_up_bf, w_down_bf, tm, tf, rows_per_batch):
    m, d = x1.shape
    ff = w_up_bf.shape[1]
    assert ff // tf >= 2, "the first and last hidden chunks take different branches"
    tiles_per_batch = rows_per_batch // tm
    vec = pl.BlockSpec((1, d), lambda i, k: (0, 0))
    sub = min(tm, MLP_SUB_M)
    vmem_estimate = (2 * 2 * tm * d * 4 + tm * d * 2 + 2 * 2 * d * tf * 2
                     + sub * (tf * 4 + tf * 2 + d * 4) + VMEM_COMPILER_ALLOWANCE)
    return pl.pallas_call(
        _mlp_kernel,
        grid=(m // tm, ff // tf),
        in_specs=[pl.BlockSpec((tm, d), lambda i, k: (i, 0)),
                  pl.BlockSpec((1,) + mod3.shape[1:], lambda i, k: (i // tiles_per_batch, 0, 0)),
                  vec, vec,
                  pl.BlockSpec((d, tf), lambda i, k: (0, k)),
                  pl.BlockSpec((tf, d), lambda i, k: (k, 0))],
        out_specs=pl.BlockSpec((tm, d), lambda i, k: (i, 0)),
        out_shape=jax.ShapeDtypeStruct((m, d), F32),
        scratch_shapes=[pltpu.VMEM((tm, d), BF16)],
        compiler_params=_params(2, max(vmem_estimate, VMEM_LIMIT_BYTES)),
        name="mlp",
    )(x1, mod3, prew, postw, w_up_bf, w_down_bf)


def _tile(n, target):
    t = min(n, target)
    assert n % t == 0, (n, t)
    return t


def kernel(x, c, ctx, c_ctx, w_ada, b_ada, mix_pre_w, mix_post_w, mlp_pre_w, mlp_post_w, w_in, conv_w, conv_b,
           conv_ln_w, conv_ln_b, rec_lb_logits, rec_norm_w, w_out, w_up, w_down):
    bsz, seq, d = x.shape
    assert w_in.shape[0] == 1, "single-layer block"
    gw = d // 2
    assert gw % REC_DK == 0 and seq % REC_CHUNK == 0 and ctx.shape[1] % SUBLANES == 0
    m = bsz * seq

    cc = jnp.concatenate([c, c_ctx[None, :], jnp.zeros((SUBLANES - bsz - 1, d), c.dtype)], axis=0)
    mod_a = _ada(cc, w_ada[0], b_ada, _tile(2 * d, 1024), 2 * d).reshape(SUBLANES, 2, d)

    lbl = rec_lb_logits.astype(F32)

    s0f, s0b = _ctx_states(ctx, mod_a, mix_pre_w, lbl, w_in[0], col0=2 * gw + 2 * gw,
                           heads_per_step=min(4, gw // REC_DK))

    x2 = x.reshape(m, d)
    tm = _tile(seq, 512)
    z2 = _inproj(x2, mod_a, mix_pre_w, lbl, w_in[0], _tile(seq, 1024), _tile(gw, 512), seq)
    z3 = z2.reshape(z2.shape[0], bsz, seq, gw)

    tl = _tile(seq, 512)
    uconv, mod_b, (w_out_bf, w_up_bf, w_down_bf) = _conv(
        z3, conv_w[0], conv_b, conv_ln_w, conv_ln_b, tl, _tile(tl, 64), [w_out[0], w_up[0], w_down[0]],
        cc, w_ada[0], b_ada, ada_col0=2 * d)
    mod_b = mod_b.reshape(SUBLANES, 4, d)
    o_f, o_b = _gla(z3, s0f, s0b, _tile(seq, 512), gw)

    x1 = _outproj(uconv.reshape(m, gw), o_f.reshape(m, gw), o_b.reshape(m, gw), z2, x2, mod_b,
                  rec_norm_w, mix_post_w, w_out_bf, _tile(seq, 512), seq)
    out = _mlp(x1, mod_b, mlp_pre_w, mlp_post_w, w_up_bf, w_down_bf, _tile(seq, 1024), _tile(w_up.shape[2], 1024), seq)
    return out.reshape(bsz, seq, d)
```

```python
import functools

import jax
import jax.numpy as jnp
from jax import lax
from jax.experimental import pallas as pl
from jax.experimental.pallas import tpu as pltpu

NORM_EPS = 1e-6
REC_DK = 128
REC_CHUNK = 64
TRI_ROWS = 256
CONV_K = 31
CONV_HALO = 16
CONV_SLAB = 256
SUBLANES = 8
PROJ_SUB_M = 512
MLP_SUB_M = 512
V7X_VMEM_BYTES = 64 * 1024 * 1024
VMEM_LIMIT_BYTES = 56 * 1024 * 1024
VMEM_COMPILER_ALLOWANCE = 3 * 1024 * 1024

F32 = jnp.float32
BF16 = jnp.bfloat16


def _params(n_axes, vmem_limit_bytes=VMEM_LIMIT_BYTES):
    assert vmem_limit_bytes < V7X_VMEM_BYTES
    return pltpu.CompilerParams(dimension_semantics=("arbitrary",) * n_axes,
                                vmem_limit_bytes=vmem_limit_bytes)


def _row_tiles(rows, sub):
    sub = min(rows, sub)
    assert rows % sub == 0, (rows, sub)
    return [slice(r0, r0 + sub) for r0 in range(0, rows, sub)]


def _sigmoid(x):
    return 1.0 / (1.0 + jnp.exp(-x))


def _rms_norm(xf, w):
    ms = jnp.mean(xf * xf, axis=-1, keepdims=True)
    return xf * lax.rsqrt(ms + NORM_EPS) * w


def _dot(a, b):
    return jnp.dot(a, b, preferred_element_type=F32)


def _dot_nt(a, b):
    return lax.dot_general(a, b, (((1,), (1,)), ((), ())), preferred_element_type=F32)


def _dot_tn(a, b):
    return lax.dot_general(a, b, (((0,), (0,)), ((), ())), preferred_element_type=F32)


def _tri_sum(tri, g):
    if g.dtype == BF16:
        return _dot(tri, g)
    g_hi = g.astype(BF16)
    g_lo = (g - g_hi.astype(F32)).astype(BF16)
    return _dot(tri, g_hi) + _dot(tri, g_lo)


def _lower_bound(lbl, direction):
    m = jnp.max(lbl, axis=0)
    e = jnp.exp(lbl - m[None])
    sm0 = e[0] / jnp.sum(e, axis=0)
    return sm0[direction:direction + 1]


def _log_forget(z, lb):
    return jnp.log2(lb + (1.0 - lb) * _sigmoid(z))


def _ada_kernel(c_ref, w_ref, b_ref, o_ref):
    c = c_ref[...]
    s = (c * _sigmoid(c)).astype(BF16)
    o_ref[...] = _dot(s, w_ref[...].astype(BF16)) + b_ref[...]


def _ada(cc, w_ada, b_ada, tn, n_cols):
    rows, d = cc.shape
    return pl.pallas_call(
        _ada_kernel,
        grid=(n_cols // tn,),
        in_specs=[pl.BlockSpec((rows, d), lambda j: (0, 0)),
                  pl.BlockSpec((d, tn), lambda j: (0, j)),
                  pl.BlockSpec((1, tn), lambda j: (0, j))],
        out_specs=pl.BlockSpec((rows, tn), lambda j: (0, j)),
        out_shape=jax.ShapeDtypeStruct((rows, n_cols), F32),
        compiler_params=_params(1),
        name="ada",
    )(cc, w_ada, b_ada)


def _ctx_kernel(ctx_ref, mod_ref, npw_ref, lbl_ref, wi_ref, wf_ref, wb_ref, sf_ref, sb_ref, h_scr):
    @pl.when(pl.program_id(1) == 0)
    def _():
        h = _rms_norm(ctx_ref[0], npw_ref[...])
        h_scr[...] = (h * (1.0 + mod_ref[0, 1:2, :]) + mod_ref[0, 0:1, :]).astype(BF16)

    hb = h_scr[...]
    lc = hb.shape[0]
    v = _dot(hb, wi_ref[...].astype(BF16)).astype(BF16)
    lbl = lbl_ref[...]
    g_f = _log_forget(_dot(hb, wf_ref[...].astype(BF16)), _lower_bound(lbl, 0))
    g_b = _log_forget(_dot(hb, wb_ref[...].astype(BF16)), _lower_bound(lbl, 1))
    row = lax.broadcasted_iota(jnp.int32, (lc, lc), 0)
    col = lax.broadcasted_iota(jnp.int32, (lc, lc), 1)
    incl = (col <= row).astype(BF16)
    excl = (col < row).astype(BF16)
    b_f = _tri_sum(incl, g_f)
    kd_f = ((1.0 - jnp.exp2(g_f)) * jnp.exp2(b_f[lc - 1:lc, :] - b_f)).astype(BF16)
    c_b = _tri_sum(excl, g_b)
    kd_b = ((1.0 - jnp.exp2(g_b)) * jnp.exp2(c_b)).astype(BF16)
    for h in range(v.shape[1] // REC_DK):
        ls = slice(h * REC_DK, (h + 1) * REC_DK)
        sf_ref[0, h] = _dot_tn(v[:, ls], kd_f[:, ls])
        sb_ref[0, h] = _dot_tn(v[:, ls], kd_b[:, ls])


def _ctx_states(ctx, mod3, npw, lbl, w_in, col0, heads_per_step):
    bsz, lc, d = ctx.shape
    nh = lbl.shape[-1] // REC_DK
    hg = heads_per_step
    wcols = hg * REC_DK
    cb = col0 // wcols
    ng = nh // hg
    wspec = lambda off: pl.BlockSpec((d, wcols), lambda b, g, off=off: (0, cb + off * ng + g))
    st_spec = pl.BlockSpec((1, hg, REC_DK, REC_DK), lambda b, g: (b, g, 0, 0))
    st_shape = jax.ShapeDtypeStruct((bsz, nh, REC_DK, REC_DK), F32)
    return pl.pallas_call(
        _ctx_kernel,
        grid=(bsz, ng),
        in_specs=[pl.BlockSpec((1, lc, d), lambda b, g: (b, 0, 0)),
                  pl.BlockSpec((1,) + mod3.shape[1:], lambda b, g: (bsz, 0, 0)),
                  pl.BlockSpec((1, d), lambda b, g: (0, 0)),
                  pl.BlockSpec(lbl.shape[:2] + (wcols,), lambda b, g: (0, 0, g)),
                  wspec(0), wspec(1), wspec(2)],
        out_specs=[st_spec, st_spec],
        out_shape=[st_shape, st_shape],
        scratch_shapes=[pltpu.VMEM((lc, d), BF16)],
        compiler_params=_params(2),
        name="ctx_states",
    )(ctx, mod3, npw, lbl, w_in, w_in, w_in)


def _project_tiles(h_scr, w_ref, wb_scr, tn, epilogue):
    for rs in _row_tiles(h_scr.shape[0], PROJ_SUB_M):
        for n0 in range(0, w_ref.shape[1], tn):
            cs = slice(n0, n0 + tn)
            if rs.start == 0:
                wb_scr[:, cs] = w_ref[:, cs].astype(BF16)
            epilogue(rs, cs, _dot(h_scr[rs, :], wb_scr[:, cs]))


def _inproj_kernel(x_ref, mod_ref, npw_ref, lbl_ref, w_ref, o_ref, h_scr, a_scr, wb_scr, *, tn):
    j = pl.program_id(1)

    @pl.when(j == 0)
    def _():
        h = _rms_norm(x_ref[...], npw_ref[...])
        h_scr[...] = (h * (1.0 + mod_ref[0, 1:2, :]) + mod_ref[0, 0:1, :]).astype(BF16)

        def park(rs, cs, z):
            a_scr[rs, cs] = z
        _project_tiles(h_scr, w_ref, wb_scr, tn, park)

    @pl.when(j == 1)
    def _():
        def glu(rs, cs, z):
            o_ref[rs, cs] = (a_scr[rs, cs] * _sigmoid(z)).astype(o_ref.dtype)
        _project_tiles(h_scr, w_ref, wb_scr, tn, glu)

    @pl.when((j == 2) | (j == 3))
    def _():
        def swish(rs, cs, z):
            o_ref[rs, cs] = (z * _sigmoid(z)).astype(o_ref.dtype)
        _project_tiles(h_scr, w_ref, wb_scr, tn, swish)

    @pl.when(j == 4)
    def _():
        def values(rs, cs, z):
            o_ref[rs, cs] = z.astype(o_ref.dtype)
        _project_tiles(h_scr, w_ref, wb_scr, tn, values)

    @pl.when(j >= 5)
    def _():
        lbl = lbl_ref[...]
        lb = jnp.where(j == 5, _lower_bound(lbl, 0), _lower_bound(lbl, 1))

        def log_gate(rs, cs, z):
            o_ref[rs, cs] = _log_forget(z, lb[:, cs]).astype(o_ref.dtype)
        _project_tiles(h_scr, w_ref, wb_scr, tn, log_gate)


def _inproj(x2, mod3, npw, lbl, w_in, tm, tn, rows_per_batch):
    m, d = x2.shape
    gw = d // 2
    n_groups = w_in.shape[1] // gw
    tiles_per_batch = rows_per_batch // tm
    return pl.pallas_call(
        functools.partial(_inproj_kernel, tn=tn),
        grid=(m // tm, n_groups),
        in_specs=[pl.BlockSpec((tm, d), lambda i, j: (i, 0)),
                  pl.BlockSpec((1,) + mod3.shape[1:], lambda i, j: (i // tiles_per_batch, 0, 0)),
                  pl.BlockSpec((1, d), lambda i, j: (0, 0)),
                  pl.BlockSpec(lbl.shape, lambda i, j: (0, 0, 0)),
                  pl.BlockSpec((d, gw), lambda i, j: (0, j))],
        out_specs=pl.BlockSpec((None, tm, gw), lambda i, j: (jnp.maximum(j - 1, 0), i, 0)),
        out_shape=jax.ShapeDtypeStruct((n_groups - 1, m, gw), BF16),
        scratch_shapes=[pltpu.VMEM((tm, d), BF16), pltpu.VMEM((tm, gw), F32), pltpu.VMEM((d, gw), BF16)],
        compiler_params=_params(2),
        name="inproj",
    )(x2, mod3, npw, lbl, w_in)


def _conv_kernel(u_ref, ul_ref, ur_ref, cw_ref, cb_ref, lnw_ref, lnb_ref, cc_ref, wada_ref, bada_ref, *rest,
                 rb, n_cast):
    cast_in, o_ref, modb_ref = rest[:n_cast], rest[n_cast], rest[n_cast + 1]
    cast_out = rest[n_cast + 2:2 * n_cast + 2]
    buf, y_scr, wb_scr = rest[2 * n_cast + 2:]
    for src, dst in zip(cast_in, cast_out):
        dst[...] = src[...].astype(dst.dtype)
    _ada_kernel(cc_ref, wada_ref, bada_ref, modb_ref)

    t = pl.program_id(1)
    tl, ch = u_ref.shape[1], u_ref.shape[2]

    @pl.when((pl.program_id(0) == 0) & (t == 0))
    def _():
        for k in range(CONV_K):
            wb_scr[k] = jnp.broadcast_to(cw_ref[k:k + 1, :], (SUBLANES, ch))

    buf[pl.ds(CONV_HALO, tl), :] = u_ref[0]
    buf[pl.ds(0, CONV_HALO), :] = jnp.where(t > 0, ul_ref[0], jnp.zeros_like(ul_ref[0]))
    buf[pl.ds(CONV_HALO + tl, CONV_HALO), :] = jnp.where(t < pl.num_programs(1) - 1, ur_ref[0],
                                                          jnp.zeros_like(ur_ref[0]))

    shift0 = CONV_HALO - CONV_K // 2
    win = rb + 2 * CONV_HALO
    span = win - SUBLANES
    r_i = lax.broadcasted_iota(jnp.int32, (SUBLANES * span, win), 0)
    c_i = lax.broadcasted_iota(jnp.int32, (SUBLANES * span, win), 1)
    shifts = (c_i == r_i % span + r_i // span).astype(BF16)

    def row_block(i, carry):
        r0 = pl.multiple_of(i * rb, rb)

        for l0 in range(0, ch, CONV_SLAB):
            ls = slice(l0, l0 + CONV_SLAB)
            shifted = _dot(shifts, buf[pl.ds(r0, win), ls])
            acc = [jnp.zeros((SUBLANES, CONV_SLAB), F32) for _ in range(rb // SUBLANES)]
            for p in range(SUBLANES):
                for k in range(CONV_K):
                    if (k + shift0) % SUBLANES == p:
                        a0 = p * span + (k + shift0) // SUBLANES * SUBLANES
                        w8 = wb_scr[k, :, ls]
                        for g in range(rb // SUBLANES):
                            acc[g] = acc[g] + shifted[a0 + g * SUBLANES:a0 + (g + 1) * SUBLANES, :] * w8
            y_scr[pl.ds(r0, rb), ls] = jnp.concatenate(acc, axis=0)
        return carry

    lax.fori_loop(0, tl // rb, row_block, 0, unroll=True)

    y = y_scr[...] + cb_ref[...]
    mu = jnp.mean(y, axis=-1, keepdims=True)
    yc = y - mu
    var = jnp.mean(yc * yc, axis=-1, keepdims=True)
    yn = yc * lax.rsqrt(var + NORM_EPS) * lnw_ref[...] + lnb_ref[...]
    o_ref[0] = (yn * _sigmoid(yn)).astype(o_ref.dtype)


def _conv(z3, conv_w, conv_b, ln_w, ln_b, tl, rb, cast_weights, cc, w_ada, b_ada, ada_col0):
    _, bsz, seq, _ = z3.shape
    ch = conv_w.shape[1]
    hb = tl // CONV_HALO
    n_halo = seq // CONV_HALO
    nt = seq // tl
    n_steps = bsz * nt
    vec = pl.BlockSpec((1, ch), lambda b, t: (0, 0))
    for w in cast_weights:
        assert w.shape[0] % (n_steps * 16) == 0, w.shape
    cast_specs = [pl.BlockSpec((w.shape[0] // n_steps, w.shape[1]), lambda b, t: (b * nt + t, 0))
                  for w in cast_weights]
    n_ada = w_ada.shape[1] - ada_col0
    ta = n_ada // n_steps
    assert n_ada % n_steps == 0 and ta % 128 == 0 and ada_col0 % ta == 0
    ada_specs = [pl.BlockSpec(cc.shape, lambda b, t: (0, 0)),
                 pl.BlockSpec((w_ada.shape[0], ta), lambda b, t: (0, ada_col0 // ta + b * nt + t)),
                 pl.BlockSpec((1, ta), lambda b, t: (0, ada_col0 // ta + b * nt + t))]
    outs = pl.pallas_call(
        functools.partial(_conv_kernel, rb=rb, n_cast=len(cast_weights)),
        grid=(bsz, nt),
        in_specs=[pl.BlockSpec((None, 1, tl, ch), lambda b, t: (0, b, t, 0)),
                  pl.BlockSpec((None, 1, CONV_HALO, ch), lambda b, t: (0, b, jnp.maximum(t * hb - 1, 0), 0)),
                  pl.BlockSpec((None, 1, CONV_HALO, ch),
                               lambda b, t: (0, b, jnp.minimum((t + 1) * hb, n_halo - 1), 0)),
                  pl.BlockSpec((CONV_K, ch), lambda b, t: (0, 0)),
                  vec, vec, vec] + ada_specs + cast_specs,
        out_specs=[pl.BlockSpec((1, tl, ch), lambda b, t: (b, t, 0)),
                   pl.BlockSpec((cc.shape[0], ta), lambda b, t: (0, b * nt + t))] + cast_specs,
        out_shape=[jax.ShapeDtypeStruct((bsz, seq, ch), BF16), jax.ShapeDtypeStruct((cc.shape[0], n_ada), F32)]
                  + [jax.ShapeDtypeStruct(w.shape, BF16) for w in cast_weights],
        scratch_shapes=[pltpu.VMEM((tl + 2 * CONV_HALO, ch), BF16), pltpu.VMEM((tl, ch), F32),
                        pltpu.VMEM((CONV_K, SUBLANES, ch), F32)],
        compiler_params=_params(2),
        name="conv",
    )(z3, z3, z3, conv_w, conv_b, ln_w, ln_b, cc, w_ada, b_ada, *cast_weights)
    return outs[0], outs[1], outs[2:]


class _Scan:
    def __init__(self, q_ref, v_ref, g_ref, o_ref, st, scr, backward):
        self.q_ref, self.v_ref, self.g_ref, self.o_ref, self.st = q_ref, v_ref, g_ref, o_ref, st
        self.kd_s, self.qa_s, self.ka_s, self.qe_s, self.dec_s, self.kv_s, self.sb_s = scr
        rows, width = q_ref.shape[1], q_ref.shape[2]
        self.n_chunks, self.n_heads = rows // REC_CHUNK, width // REC_DK
        span = min(rows, TRI_ROWS)
        r_i = lax.broadcasted_iota(jnp.int32, (span, span), 0)
        c_i = lax.broadcasted_iota(jnp.int32, (span, span), 1)
        same_chunk = (r_i // REC_CHUNK) == (c_i // REC_CHUNK)
        tri = (same_chunk & ((c_i >= r_i) if backward else (c_i <= r_i))).astype(BF16)
        m_r = lax.broadcasted_iota(jnp.int32, (REC_CHUNK, REC_CHUNK), 0)
        m_c = lax.broadcasted_iota(jnp.int32, (REC_CHUNK, REC_CHUNK), 1)
        self.mask = (m_c >= m_r) if backward else (m_c <= m_r)
        ref_f = REC_CHUNK // 2 - 1
        self.last, self.ref = (0, REC_CHUNK - 1 - ref_f) if backward else (REC_CHUNK - 1, ref_f)
        self.order = list(range(self.n_chunks - 1, -1, -1) if backward else range(self.n_chunks))
        self.b = jnp.concatenate([_tri_sum(tri, g_ref[0, rs, :]) for rs in _row_tiles(rows, span)], axis=0)
        self.scores = {}

    def _rows(self, c):
        return slice(c * REC_CHUNK, (c + 1) * REC_CHUNK)

    def _heads(self):
        return [(h, slice(h * REC_DK, (h + 1) * REC_DK)) for h in range(self.n_heads)]

    def prep(self, c):
        rs = self._rows(c)
        bc, qc = self.b[rs], self.q_ref[0, rs, :].astype(F32)
        b_last = bc[self.last:self.last + 1, :]
        b_ref = bc[self.ref:self.ref + 1, :]
        k = 1.0 - jnp.exp2(self.g_ref[0, rs, :].astype(F32))
        ka = k * jnp.exp2(b_ref - bc)
        qa = qc * jnp.exp2(bc - b_ref)
        self.ka_s[rs, :] = ka.astype(BF16)
        self.qa_s[rs, :] = qa.astype(BF16)
        self.kd_s[rs, :] = (ka * jnp.exp2(b_last - b_ref)).astype(BF16)
        self.qe_s[rs, :] = (qa * jnp.exp2(b_ref)).astype(BF16)
        self.dec_s[c:c + 1, :] = jnp.exp2(b_last)

    def intra(self, c):
        rs = self._rows(c)
        for h, ls in self._heads():
            s = _dot_nt(self.qa_s[rs, ls], self.ka_s[rs, ls])
            self.scores[c, h] = jnp.where(self.mask, s, 0.0).astype(BF16)
            self.kv_s[c, h] = _dot_tn(self.v_ref[0, rs, ls], self.kd_s[rs, ls])

    def recur(self, c):
        for h, ls in self._heads():
            s = self.st[h]
            self.sb_s[c, h] = s.astype(BF16)
            self.st[h] = self.dec_s[c:c + 1, ls] * s + self.kv_s[c, h]

    def out(self, c):
        rs = self._rows(c)
        for h, ls in self._heads():
            self.o_ref[0, rs, ls] = (_dot(self.scores[c, h], self.v_ref[0, rs, ls])
                                     + _dot_nt(self.qe_s[rs, ls], self.sb_s[c, h]))


def _gla_kernel(qf_ref, vf_ref, gf_ref, qb_ref, vb_ref, gb_ref, s0f_ref, s0b_ref,
                of_ref, ob_ref, stf, stb, *scr):
    @pl.when(pl.program_id(1) == 0)
    def _():
        stf[...] = s0f_ref[0]
        stb[...] = s0b_ref[0]

    half = len(scr) // 2
    fwd = _Scan(qf_ref, vf_ref, gf_ref, of_ref, stf, scr[:half], backward=False)
    bwd = _Scan(qb_ref, vb_ref, gb_ref, ob_ref, stb, scr[half:], backward=True)
    for cf, cb in zip(fwd.order, bwd.order):
        for scan, c in ((fwd, cf), (bwd, cb)):
            scan.prep(c)
        for scan, c in ((fwd, cf), (bwd, cb)):
            scan.intra(c)
        for scan, c in ((fwd, cf), (bwd, cb)):
            scan.recur(c)
            scan.out(c)


def _gla(z3, s0f, s0b, rows, width):
    _, bsz, seq, _ = z3.shape
    nb = seq // rows
    nh = width // REC_DK
    nc = rows // REC_CHUNK
    state = pltpu.VMEM((nh, REC_DK, REC_DK), F32)
    per_direction = ([pltpu.VMEM((rows, width), BF16)] * 4
                     + [pltpu.VMEM((max(nc, SUBLANES), width), F32),
                        pltpu.VMEM((nc, nh, REC_DK, REC_DK), F32),
                        pltpu.VMEM((nc, nh, REC_DK, REC_DK), BF16)])
    fwd = lambda col: pl.BlockSpec((None, 1, rows, width), lambda b, n, col=col: (col, b, n, 0))
    bwd = lambda col: pl.BlockSpec((None, 1, rows, width), lambda b, n, col=col: (col, b, nb - 1 - n, 0))
    st_spec = pl.BlockSpec((1, nh, REC_DK, REC_DK), lambda b, n: (b, 0, 0, 0))
    o_shape = jax.ShapeDtypeStruct((bsz, seq, width), F32)
    return pl.pallas_call(
        _gla_kernel,
        grid=(bsz, nb),
        in_specs=[fwd(1), fwd(3), fwd(4), bwd(1), bwd(3), bwd(5), st_spec, st_spec],
        out_specs=[pl.BlockSpec((1, rows, width), lambda b, n: (b, n, 0)),
                   pl.BlockSpec((1, rows, width), lambda b, n: (b, nb - 1 - n, 0))],
        out_shape=[o_shape, o_shape],
        scratch_shapes=[state, state] + per_direction * 2,
        compiler_params=_params(2),
        name="gla",
    )(z3, z3, z3, z3, z3, z3, s0f, s0b)


def _outproj_kernel(u_ref, of_ref, ob_ref, gs_ref, x_ref, mod_ref, rnw_ref, postw_ref, w_ref, x1_ref):
    gw = u_ref.shape[1]
    o = of_ref[...] + ob_ref[...]
    heads = []
    for h in range(gw // REC_DK):
        oh = o[:, h * REC_DK:(h + 1) * REC_DK]
        heads.append(oh * lax.rsqrt(jnp.mean(oh * oh, axis=-1, keepdims=True) + NORM_EPS))
    on = jnp.concatenate(heads, axis=-1) * rnw_ref[...] * gs_ref[...]
    y = _dot(u_ref[...].astype(BF16), w_ref[0:gw, :]) + _dot(on.astype(BF16), w_ref[gw:2 * gw, :])
    x1_ref[...] = x_ref[...] + mod_ref[0, 0:1, :] * _rms_norm(y, postw_ref[...])


def _outproj(uconv, o_f, o_b, z2, x2, mod3, rnw, postw, w_out_bf, tm, rows_per_batch):
    m, d = x2.shape
    gw = d // 2
    tiles_per_batch = rows_per_batch // tm
    half = pl.BlockSpec((tm, gw), lambda i: (i, 0))
    full = pl.BlockSpec((tm, d), lambda i: (i, 0))
    vec = lambda n: pl.BlockSpec((1, n), lambda i: (0, 0))
    return pl.pallas_call(
        _outproj_kernel,
        grid=(m // tm,),
        in_specs=[half, half, half,
                  pl.BlockSpec((None, tm, gw), lambda i: (2, i, 0)),
                  full,
                  pl.BlockSpec((1,) + mod3.shape[1:], lambda i: (i // tiles_per_batch, 0, 0)),
                  vec(gw), vec(d),
                  pl.BlockSpec((d, d), lambda i: (0, 0))],
        out_specs=full,
        out_shape=jax.ShapeDtypeStruct((m, d), F32),
        compiler_params=_params(1),
        name="outproj",
    )(uconv, o_f, o_b, z2, x2, mod3, rnw, postw, w_out_bf)


def _mlp_kernel(x1_ref, mod_ref, prew_ref, postw_ref, wu_ref, wd_ref, o_ref, h_scr):
    k = pl.program_id(1)
    last = pl.num_programs(1) - 1

    def partial_sum(rs):
        hid = jnp.maximum(_dot(h_scr[rs, :], wu_ref[...]), 0.0)
        return _dot((hid * hid).astype(BF16), wd_ref[...])

    @pl.when(k == 0)
    def _():
        for rs in _row_tiles(h_scr.shape[0], MLP_SUB_M):
            h = _rms_norm(x1_ref[rs, :], prew_ref[...])
            h_scr[rs, :] = (h * (1.0 + mod_ref[0, 2:3, :]) + mod_ref[0, 1:2, :]).astype(BF16)
            o_ref[rs, :] = partial_sum(rs)

    @pl.when((k > 0) & (k < last))
    def _():
        for rs in _row_tiles(h_scr.shape[0], MLP_SUB_M):
            o_ref[rs, :] += partial_sum(rs)

    @pl.when(k == last)
    def _():
        for rs in _row_tiles(h_scr.shape[0], MLP_SUB_M):
            y = o_ref[rs, :] + partial_sum(rs)
            o_ref[rs, :] = x1_ref[rs, :] + mod_ref[0, 3:4, :] * _rms_norm(y, postw_ref[...])


def _mlp(x1, mod3, prew, postw, w_up_bf, w_down_bf, tm, tf, rows_per_batch):
    m, d = x1.shape
    ff = w_up_bf.shape[1]
    assert ff // tf >= 2, "the first and last hidden chunks take different branches"
    tiles_per_batch = rows_per_batch // tm
    vec = pl.BlockSpec((1, d), lambda i, k: (0, 0))
    sub = min(tm, MLP_SUB_M)
    vmem_estimate = (2 * 2 * tm * d * 4 + tm * d * 2 + 2 * 2 * d * tf * 2
                     + sub * (tf * 4 + tf * 2 + d * 4) + VMEM_COMPILER_ALLOWANCE)
    return pl.pallas_call(
        _mlp_kernel,
        grid=(m // tm, ff // tf),
        in_specs=[pl.BlockSpec((tm, d), lambda i, k: (i, 0)),
                  pl.BlockSpec((1,) + mod3.shape[1:], lambda i, k: (i // tiles_per_batch, 0, 0)),
                  vec, vec,
                  pl.BlockSpec((d, tf), lambda i, k: (0, k)),
                  pl.BlockSpec((tf, d), lambda i, k: (k, 0))],
        out_specs=pl.BlockSpec((tm, d), lambda i, k: (i, 0)),
        out_shape=jax.ShapeDtypeStruct((m, d), F32),
        scratch_shapes=[pltpu.VMEM((tm, d), BF16)],
        compiler_params=_params(2, max(vmem_estimate, VMEM_LIMIT_BYTES)),
        name="mlp",
    )(x1, mod3, prew, postw, w_up_bf, w_down_bf)


def _tile(n, target):
    t = min(n, target)
    assert n % t == 0, (n, t)
    return t


def kernel(x, c, ctx, c_ctx, w_ada, b_ada, mix_pre_w, mix_post_w, mlp_pre_w, mlp_post_w, w_in, conv_w, conv_b,
           conv_ln_w, conv_ln_b, rec_lb_logits, rec_norm_w, w_out, w_up, w_down):
    bsz, seq, d = x.shape
    assert w_in.shape[0] == 1, "single-layer block"
    gw = d // 2
    assert gw % REC_DK == 0 and seq % REC_CHUNK == 0 and ctx.shape[1] % SUBLANES == 0
    m = bsz * seq

    cc = jnp.concatenate([c, c_ctx[None, :], jnp.zeros((SUBLANES - bsz - 1, d), c.dtype)], axis=0)
    mod_a = _ada(cc, w_ada[0], b_ada, _tile(2 * d, 1024), 2 * d).reshape(SUBLANES, 2, d)

    lbl = rec_lb_logits.astype(F32)

    s0f, s0b = _ctx_states(ctx, mod_a, mix_pre_w, lbl, w_in[0], col0=2 * gw + 2 * gw,
                           heads_per_step=min(4, gw // REC_DK))

    x2 = x.reshape(m, d)
    tm = _tile(seq, 512)
    z2 = _inproj(x2, mod_a, mix_pre_w, lbl, w_in[0], _tile(seq, 1024), _tile(gw, 512), seq)
    z3 = z2.reshape(z2.shape[0], bsz, seq, gw)

    tl = _tile(seq, 512)
    uconv, mod_b, (w_out_bf, w_up_bf, w_down_bf) = _conv(
        z3, conv_w[0], conv_b, conv_ln_w, conv_ln_b, tl, _tile(tl, 64), [w_out[0], w_up[0], w_down[0]],
        cc, w_ada[0], b_ada, ada_col0=2 * d)
    mod_b = mod_b.reshape(SUBLANES, 4, d)
    o_f, o_b = _gla(z3, s0f, s0b, _tile(seq, 512), gw)

    x1 = _outproj(uconv.reshape(m, gw), o_f.reshape(m, gw), o_b.reshape(m, gw), z2, x2, mod_b,
                  rec_norm_w, mix_post_w, w_out_bf, _tile(seq, 512), seq)
    out = _mlp(x1, mod_b, mlp_pre_w, mlp_post_w, w_up_bf, w_down_bf, _tile(seq, 1024), _tile(w_up.shape[2], 1024), seq)
    return out.reshape(bsz, seq, d)
```

```python
import functools

import jax
import jax.numpy as jnp
from jax import lax
from jax.experimental import pallas as pl
from jax.experimental.pallas import tpu as pltpu

NORM_EPS = 1e-6
REC_DK = 128
REC_CHUNK = 64
TRI_ROWS = 256
SUBLANES = 8
BF16_SUBLANES = 16
CONV_K = 31
CONV_HALO = BF16_SUBLANES
CONV_SLAB = 256
PROJ_SUB_M = 512
MLP_SUB_M = 512
V7X_VMEM_BYTES = 64 * 1024 * 1024
VMEM_LIMIT_BYTES = 56 * 1024 * 1024
VMEM_COMPILER_ALLOWANCE = 3 * 1024 * 1024

F32 = jnp.float32
BF16 = jnp.bfloat16


def _params(n_axes, vmem_limit_bytes=VMEM_LIMIT_BYTES):
    assert vmem_limit_bytes < V7X_VMEM_BYTES
    return pltpu.CompilerParams(dimension_semantics=("arbitrary",) * n_axes,
                                vmem_limit_bytes=vmem_limit_bytes)


def _row_tiles(rows, sub):
    sub = min(rows, sub)
    assert rows % sub == 0, (rows, sub)
    return [slice(r0, r0 + sub) for r0 in range(0, rows, sub)]


def _sigmoid(x):
    return 1.0 / (1.0 + jnp.exp(-x))


def _rms_norm(xf, w):
    ms = jnp.mean(xf * xf, axis=-1, keepdims=True)
    return xf * lax.rsqrt(ms + NORM_EPS) * w


def _dot(a, b):
    return jnp.dot(a, b, preferred_element_type=F32)


def _dot_nt(a, b):
    return lax.dot_general(a, b, (((1,), (1,)), ((), ())), preferred_element_type=F32)


def _dot_tn(a, b):
    return lax.dot_general(a, b, (((0,), (0,)), ((), ())), preferred_element_type=F32)


def _tri_sum(tri, g):
    if g.dtype == BF16:
        return _dot(tri, g)
    g_hi = g.astype(BF16)
    g_lo = (g - g_hi.astype(F32)).astype(BF16)
    return _dot(tri, g_hi) + _dot(tri, g_lo)


def _lower_bound(lbl, direction):
    m = jnp.max(lbl, axis=0)
    e = jnp.exp(lbl - m[None])
    sm0 = e[0] / jnp.sum(e, axis=0)
    return sm0[direction:direction + 1]


def _log_forget(z, lb):
    return jnp.log2(lb + (1.0 - lb) * _sigmoid(z))


def _ada_kernel(c_ref, w_ref, b_ref, o_ref):
    c = c_ref[...]
    s = (c * _sigmoid(c)).astype(BF16)
    o_ref[...] = _dot(s, w_ref[...].astype(BF16)) + b_ref[...]


def _ada(cc, w_ada, b_ada, tn, n_cols):
    rows, d = cc.shape
    return pl.pallas_call(
        _ada_kernel,
        grid=(n_cols // tn,),
        in_specs=[pl.BlockSpec((rows, d), lambda j: (0, 0)),
                  pl.BlockSpec((d, tn), lambda j: (0, j)),
                  pl.BlockSpec((1, tn), lambda j: (0, j))],
        out_specs=pl.BlockSpec((rows, tn), lambda j: (0, j)),
        out_shape=jax.ShapeDtypeStruct((rows, n_cols), F32),
        compiler_params=_params(1),
        name="ada",
    )(cc, w_ada, b_ada)


def _ctx_kernel(ctx_ref, mod_ref, npw_ref, lbl_ref, wi_ref, wf_ref, wb_ref, sf_ref, sb_ref, h_scr):
    @pl.when(pl.program_id(1) == 0)
    def _():
        h = _rms_norm(ctx_ref[0], npw_ref[...])
        h_scr[...] = (h * (1.0 + mod_ref[0, 1:2, :]) + mod_ref[0, 0:1, :]).astype(BF16)

    hb = h_scr[...]
    lc = hb.shape[0]
    v = _dot(hb, wi_ref[...].astype(BF16)).astype(BF16)
    lbl = lbl_ref[...]
    g_f = _log_forget(_dot(hb, wf_ref[...].astype(BF16)), _lower_bound(lbl, 0))
    g_b = _log_forget(_dot(hb, wb_ref[...].astype(BF16)), _lower_bound(lbl, 1))
    row = lax.broadcasted_iota(jnp.int32, (lc, lc), 0)
    col = lax.broadcasted_iota(jnp.int32, (lc, lc), 1)
    incl = (col <= row).astype(BF16)
    excl = (col < row).astype(BF16)
    b_f = _tri_sum(incl, g_f)
    kd_f = ((1.0 - jnp.exp2(g_f)) * jnp.exp2(b_f[lc - 1:lc, :] - b_f)).astype(BF16)
    c_b = _tri_sum(excl, g_b)
    kd_b = ((1.0 - jnp.exp2(g_b)) * jnp.exp2(c_b)).astype(BF16)
    for h in range(v.shape[1] // REC_DK):
        ls = slice(h * REC_DK, (h + 1) * REC_DK)
        sf_ref[0, h] = _dot_tn(v[:, ls], kd_f[:, ls])
        sb_ref[0, h] = _dot_tn(v[:, ls], kd_b[:, ls])


def _ctx_states(ctx, mod3, npw, lbl, w_in, col0, heads_per_step):
    bsz, lc, d = ctx.shape
    nh = lbl.shape[-1] // REC_DK
    hg = heads_per_step
    wcols = hg * REC_DK
    cb = col0 // wcols
    ng = nh // hg
    resident = dict(pipeline_mode=pl.Buffered(1)) if ng == 1 else {}
    wspec = lambda off: pl.BlockSpec((d, wcols), lambda b, g, off=off: (0, cb + off * ng + g), **resident)
    st_spec = pl.BlockSpec((1, hg, REC_DK, REC_DK), lambda b, g: (b, g, 0, 0))
    st_shape = jax.ShapeDtypeStruct((bsz, nh, REC_DK, REC_DK), F32)
    return pl.pallas_call(
        _ctx_kernel,
        grid=(bsz, ng),
        in_specs=[pl.BlockSpec((1, lc, d), lambda b, g: (b, 0, 0)),
                  pl.BlockSpec((1,) + mod3.shape[1:], lambda b, g: (bsz, 0, 0)),
                  pl.BlockSpec((1, d), lambda b, g: (0, 0)),
                  pl.BlockSpec(lbl.shape[:2] + (wcols,), lambda b, g: (0, 0, g)),
                  wspec(0), wspec(1), wspec(2)],
        out_specs=[st_spec, st_spec],
        out_shape=[st_shape, st_shape],
        scratch_shapes=[pltpu.VMEM((lc, d), BF16)],
        compiler_params=_params(2),
        name="ctx_states",
    )(ctx, mod3, npw, lbl, w_in, w_in, w_in)


def _project_tiles(h_scr, w_ref, wb_scr, tn, epilogue):
    for rs in _row_tiles(h_scr.shape[0], PROJ_SUB_M):
        for n0 in range(0, w_ref.shape[1], tn):
            cs = slice(n0, n0 + tn)
            if rs.start == 0:
                wb_scr[:, cs] = w_ref[:, cs].astype(BF16)
            epilogue(rs, cs, _dot(h_scr[rs, :], wb_scr[:, cs]))


def _inproj_kernel(x_ref, mod_ref, npw_ref, lbl_ref, w_ref, o_ref, h_scr, a_scr, wb_scr, *, tn):
    j = pl.program_id(1)

    @pl.when(j == 0)
    def _():
        h = _rms_norm(x_ref[...], npw_ref[...])
        h_scr[...] = (h * (1.0 + mod_ref[0, 1:2, :]) + mod_ref[0, 0:1, :]).astype(BF16)

        def park(rs, cs, z):
            a_scr[rs, cs] = z
        _project_tiles(h_scr, w_ref, wb_scr, tn, park)

    @pl.when(j == 1)
    def _():
        def glu(rs, cs, z):
            o_ref[rs, cs] = (a_scr[rs, cs] * _sigmoid(z)).astype(o_ref.dtype)
        _project_tiles(h_scr, w_ref, wb_scr, tn, glu)

    @pl.when((j == 2) | (j == 3))
    def _():
        def swish(rs, cs, z):
            o_ref[rs, cs] = (z * _sigmoid(z)).astype(o_ref.dtype)
        _project_tiles(h_scr, w_ref, wb_scr, tn, swish)

    @pl.when(j == 4)
    def _():
        def values(rs, cs, z):
            o_ref[rs, cs] = z.astype(o_ref.dtype)
        _project_tiles(h_scr, w_ref, wb_scr, tn, values)

    @pl.when(j >= 5)
    def _():
        lbl = lbl_ref[...]
        lb = jnp.where(j == 5, _lower_bound(lbl, 0), _lower_bound(lbl, 1))

        def log_gate(rs, cs, z):
            o_ref[rs, cs] = _log_forget(z, lb[:, cs]).astype(o_ref.dtype)
        _project_tiles(h_scr, w_ref, wb_scr, tn, log_gate)


def _inproj(x2, mod3, npw, lbl, w_in, tm, tn, rows_per_batch):
    m, d = x2.shape
    gw = d // 2
    n_groups = w_in.shape[1] // gw
    tiles_per_batch = rows_per_batch // tm
    return pl.pallas_call(
        functools.partial(_inproj_kernel, tn=tn),
        grid=(m // tm, n_groups),
        in_specs=[pl.BlockSpec((tm, d), lambda i, j: (i, 0)),
                  pl.BlockSpec((1,) + mod3.shape[1:], lambda i, j: (i // tiles_per_batch, 0, 0)),
                  pl.BlockSpec((1, d), lambda i, j: (0, 0)),
                  pl.BlockSpec(lbl.shape, lambda i, j: (0, 0, 0)),
                  pl.BlockSpec((d, gw), lambda i, j: (0, j))],
        out_specs=pl.BlockSpec((None, tm, gw), lambda i, j: (jnp.maximum(j - 1, 0), i, 0)),
        out_shape=jax.ShapeDtypeStruct((n_groups - 1, m, gw), BF16),
        scratch_shapes=[pltpu.VMEM((tm, d), BF16), pltpu.VMEM((tm, gw), F32), pltpu.VMEM((d, gw), BF16)],
        compiler_params=_params(2),
        name="inproj",
    )(x2, mod3, npw, lbl, w_in)


def _conv_kernel(u_ref, ul_ref, ur_ref, cw_ref, cb_ref, lnw_ref, lnb_ref, cc_ref, wada_ref, bada_ref, *rest,
                 rb, n_cast):
    cast_in, o_ref, modb_ref = rest[:n_cast], rest[n_cast], rest[n_cast + 1]
    cast_out = rest[n_cast + 2:2 * n_cast + 2]
    buf, y_scr, wb_scr = rest[2 * n_cast + 2:]
    for src, dst in zip(cast_in, cast_out):
        dst[...] = src[...].astype(dst.dtype)
    _ada_kernel(cc_ref, wada_ref, bada_ref, modb_ref)

    t = pl.program_id(1)
    tl, ch = u_ref.shape[1], u_ref.shape[2]

    @pl.when((pl.program_id(0) == 0) & (t == 0))
    def _():
        for k in range(CONV_K):
            wb_scr[k] = jnp.broadcast_to(cw_ref[k:k + 1, :], (SUBLANES, ch))

    buf[pl.ds(CONV_HALO, tl), :] = u_ref[0]
    buf[pl.ds(0, CONV_HALO), :] = jnp.where(t > 0, ul_ref[0], jnp.zeros_like(ul_ref[0]))
    buf[pl.ds(CONV_HALO + tl, CONV_HALO), :] = jnp.where(t < pl.num_programs(1) - 1, ur_ref[0],
                                                          jnp.zeros_like(ur_ref[0]))

    shift0 = CONV_HALO - CONV_K // 2
    win = rb + 2 * CONV_HALO
    span = win - SUBLANES
    r_i = lax.broadcasted_iota(jnp.int32, (SUBLANES * span, win), 0)
    c_i = lax.broadcasted_iota(jnp.int32, (SUBLANES * span, win), 1)
    shifts = (c_i == r_i % span + r_i // span).astype(BF16)

    def row_block(i, carry):
        r0 = pl.multiple_of(i * rb, rb)

        for l0 in range(0, ch, CONV_SLAB):
            ls = slice(l0, l0 + CONV_SLAB)
            shifted = _dot(shifts, buf[pl.ds(r0, win), ls])
            acc = [jnp.zeros((SUBLANES, CONV_SLAB), F32) for _ in range(rb // SUBLANES)]
            for p in range(SUBLANES):
                for k in range(CONV_K):
                    if (k + shift0) % SUBLANES == p:
                        a0 = p * span + (k + shift0) // SUBLANES * SUBLANES
                        w8 = wb_scr[k, :, ls]
                        for g in range(rb // SUBLANES):
                            acc[g] = acc[g] + shifted[a0 + g * SUBLANES:a0 + (g + 1) * SUBLANES, :] * w8
            y_scr[pl.ds(r0, rb), ls] = jnp.concatenate(acc, axis=0)
        return carry

    lax.fori_loop(0, tl // rb, row_block, 0, unroll=True)

    y = y_scr[...] + cb_ref[...]
    mu = jnp.mean(y, axis=-1, keepdims=True)
    yc = y - mu
    var = jnp.mean(yc * yc, axis=-1, keepdims=True)
    yn = yc * lax.rsqrt(var + NORM_EPS) * lnw_ref[...] + lnb_ref[...]
    o_ref[0] = (yn * _sigmoid(yn)).astype(o_ref.dtype)


def _conv(z3, conv_w, conv_b, ln_w, ln_b, tl, rb, cast_weights, cc, w_ada, b_ada, ada_col0):
    _, bsz, seq, _ = z3.shape
    ch = conv_w.shape[1]
    hb = tl // CONV_HALO
    n_halo = seq // CONV_HALO
    nt = seq // tl
    n_steps = bsz * nt
    vec = pl.BlockSpec((1, ch), lambda b, t: (0, 0))
    for w in cast_weights:
        assert w.shape[0] % (n_steps * BF16_SUBLANES) == 0, w.shape
    cast_specs = [pl.BlockSpec((w.shape[0] // n_steps, w.shape[1]), lambda b, t: (b * nt + t, 0))
                  for w in cast_weights]
    n_ada = w_ada.shape[1] - ada_col0
    ta = n_ada // n_steps
    assert n_ada % n_steps == 0 and ta % 128 == 0 and ada_col0 % ta == 0
    ada_specs = [pl.BlockSpec(cc.shape, lambda b, t: (0, 0)),
                 pl.BlockSpec((w_ada.shape[0], ta), lambda b, t: (0, ada_col0 // ta + b * nt + t)),
                 pl.BlockSpec((1, ta), lambda b, t: (0, ada_col0 // ta + b * nt + t))]
    outs = pl.pallas_call(
        functools.partial(_conv_kernel, rb=rb, n_cast=len(cast_weights)),
        grid=(bsz, nt),
        in_specs=[pl.BlockSpec((None, 1, tl, ch), lambda b, t: (0, b, t, 0)),
                  pl.BlockSpec((None, 1, CONV_HALO, ch), lambda b, t: (0, b, jnp.maximum(t * hb - 1, 0), 0)),
                  pl.BlockSpec((None, 1, CONV_HALO, ch),
                               lambda b, t: (0, b, jnp.minimum((t + 1) * hb, n_halo - 1), 0)),
                  pl.BlockSpec((CONV_K, ch), lambda b, t: (0, 0)),
                  vec, vec, vec] + ada_specs + cast_specs,
        out_specs=[pl.BlockSpec((1, tl, ch), lambda b, t: (b, t, 0)),
                   pl.BlockSpec((cc.shape[0], ta), lambda b, t: (0, b * nt + t))] + cast_specs,
        out_shape=[jax.ShapeDtypeStruct((bsz, seq, ch), BF16), jax.ShapeDtypeStruct((cc.shape[0], n_ada), F32)]
                  + [jax.ShapeDtypeStruct(w.shape, BF16) for w in cast_weights],
        scratch_shapes=[pltpu.VMEM((tl + 2 * CONV_HALO, ch), BF16), pltpu.VMEM((tl, ch), F32),
                        pltpu.VMEM((CONV_K, SUBLANES, ch), F32)],
        compiler_params=_params(2),
        name="conv",
    )(z3, z3, z3, conv_w, conv_b, ln_w, ln_b, cc, w_ada, b_ada, *cast_weights)
    return outs[0], outs[1], outs[2:]


class _Scan:
    def __init__(self, q_ref, v_ref, g_ref, o_ref, st, scr, backward):
        self.q_ref, self.v_ref, self.g_ref, self.o_ref, self.st = q_ref, v_ref, g_ref, o_ref, st
        self.kd_s, self.qa_s, self.ka_s, self.qe_s, self.dec_s, self.kv_s, self.sb_s = scr
        rows, width = q_ref.shape[1], q_ref.shape[2]
        self.n_chunks, self.n_heads = rows // REC_CHUNK, width // REC_DK
        span = min(rows, TRI_ROWS)
        r_i = lax.broadcasted_iota(jnp.int32, (span, span), 0)
        c_i = lax.broadcasted_iota(jnp.int32, (span, span), 1)
        same_chunk = (r_i // REC_CHUNK) == (c_i // REC_CHUNK)
        tri = (same_chunk & ((c_i >= r_i) if backward else (c_i <= r_i))).astype(BF16)
        m_r = lax.broadcasted_iota(jnp.int32, (REC_CHUNK, REC_CHUNK), 0)
        m_c = lax.broadcasted_iota(jnp.int32, (REC_CHUNK, REC_CHUNK), 1)
        self.mask = (m_c >= m_r) if backward else (m_c <= m_r)
        ref_f = REC_CHUNK // 2 - 1
        self.last, self.ref = (0, REC_CHUNK - 1 - ref_f) if backward else (REC_CHUNK - 1, ref_f)
        self.order = list(range(self.n_chunks - 1, -1, -1) if backward else range(self.n_chunks))
        self.b = jnp.concatenate([_tri_sum(tri, g_ref[0, rs, :]) for rs in _row_tiles(rows, span)], axis=0)
        self.scores = {}

    def _rows(self, c):
        return slice(c * REC_CHUNK, (c + 1) * REC_CHUNK)

    def _heads(self):
        return [(h, slice(h * REC_DK, (h + 1) * REC_DK)) for h in range(self.n_heads)]

    def prep(self, c):
        rs = self._rows(c)
        bc, qc = self.b[rs], self.q_ref[0, rs, :].astype(F32)
        b_last = bc[self.last:self.last + 1, :]
        b_ref = bc[self.ref:self.ref + 1, :]
        k = 1.0 - jnp.exp2(self.g_ref[0, rs, :].astype(F32))
        ka = k * jnp.exp2(b_ref - bc)
        qa = qc * jnp.exp2(bc - b_ref)
        self.ka_s[rs, :] = ka.astype(BF16)
        self.qa_s[rs, :] = qa.astype(BF16)
        self.kd_s[rs, :] = (ka * jnp.exp2(b_last - b_ref)).astype(BF16)
        self.qe_s[rs, :] = (qa * jnp.exp2(b_ref)).astype(BF16)
        self.dec_s[c:c + 1, :] = jnp.exp2(b_last)

    def intra(self, c):
        rs = self._rows(c)
        for h, ls in self._heads():
            s = _dot_nt(self.qa_s[rs, ls], self.ka_s[rs, ls])
            self.scores[c, h] = jnp.where(self.mask, s, 0.0).astype(BF16)
            self.kv_s[c, h] = _dot_tn(self.v_ref[0, rs, ls], self.kd_s[rs, ls])

    def recur(self, c):
        for h, ls in self._heads():
            s = self.st[h]
            self.sb_s[c, h] = s.astype(BF16)
            self.st[h] = self.dec_s[c:c + 1, ls] * s + self.kv_s[c, h]

    def out(self, c):
        rs = self._rows(c)
        for h, ls in self._heads():
            self.o_ref[0, rs, ls] = (_dot(self.scores[c, h], self.v_ref[0, rs, ls])
                                     + _dot_nt(self.qe_s[rs, ls], self.sb_s[c, h]))


def _gla_kernel(qf_ref, vf_ref, gf_ref, qb_ref, vb_ref, gb_ref, s0f_ref, s0b_ref,
                of_ref, ob_ref, stf, stb, *scr):
    @pl.when(pl.program_id(1) == 0)
    def _():
        stf[...] = s0f_ref[0]
        stb[...] = s0b_ref[0]

    half = len(scr) // 2
    fwd = _Scan(qf_ref, vf_ref, gf_ref, of_ref, stf, scr[:half], backward=False)
    bwd = _Scan(qb_ref, vb_ref, gb_ref, ob_ref, stb, scr[half:], backward=True)
    for cf, cb in zip(fwd.order, bwd.order):
        for scan, c in ((fwd, cf), (bwd, cb)):
            scan.prep(c)
        for scan, c in ((fwd, cf), (bwd, cb)):
            scan.intra(c)
        for scan, c in ((fwd, cf), (bwd, cb)):
            scan.recur(c)
            scan.out(c)


def _gla(z3, s0f, s0b, rows, width):
    _, bsz, seq, _ = z3.shape
    nb = seq // rows
    nh = width // REC_DK
    nc = rows // REC_CHUNK
    state = pltpu.VMEM((nh, REC_DK, REC_DK), F32)
    per_direction = ([pltpu.VMEM((rows, width), BF16)] * 4
                     + [pltpu.VMEM((max(nc, SUBLANES), width), F32),
                        pltpu.VMEM((nc, nh, REC_DK, REC_DK), F32),
                        pltpu.VMEM((nc, nh, REC_DK, REC_DK), BF16)])
    fwd = lambda col: pl.BlockSpec((None, 1, rows, width), lambda b, n, col=col: (col, b, n, 0))
    bwd = lambda col: pl.BlockSpec((None, 1, rows, width), lambda b, n, col=col: (col, b, nb - 1 - n, 0))
    st_spec = pl.BlockSpec((1, nh, REC_DK, REC_DK), lambda b, n: (b, 0, 0, 0))
    o_shape = jax.ShapeDtypeStruct((bsz, seq, width), F32)
    return pl.pallas_call(
        _gla_kernel,
        grid=(bsz, nb),
        in_specs=[fwd(1), fwd(3), fwd(4), bwd(1), bwd(3), bwd(5), st_spec, st_spec],
        out_specs=[pl.BlockSpec((1, rows, width), lambda b, n: (b, n, 0)),
                   pl.BlockSpec((1, rows, width), lambda b, n: (b, nb - 1 - n, 0))],
        out_shape=[o_shape, o_shape],
        scratch_shapes=[state, state] + per_direction * 2,
        compiler_params=_params(2),
        name="gla",
    )(z3, z3, z3, z3, z3, z3, s0f, s0b)


def _outproj_kernel(u_ref, of_ref, ob_ref, gs_ref, x_ref, mod_ref, rnw_ref, postw_ref, w_ref, x1_ref):
    gw = u_ref.shape[1]
    o = of_ref[...] + ob_ref[...]
    heads = []
    for h in range(gw // REC_DK):
        oh = o[:, h * REC_DK:(h + 1) * REC_DK]
        heads.append(oh * lax.rsqrt(jnp.mean(oh * oh, axis=-1, keepdims=True) + NORM_EPS))
    on = jnp.concatenate(heads, axis=-1) * rnw_ref[...] * gs_ref[...]
    y = _dot(u_ref[...].astype(BF16), w_ref[0:gw, :]) + _dot(on.astype(BF16), w_ref[gw:2 * gw, :])
    x1_ref[...] = x_ref[...] + mod_ref[0, 0:1, :] * _rms_norm(y, postw_ref[...])


def _outproj(uconv, o_f, o_b, z2, x2, mod3, rnw, postw, w_out_bf, tm, rows_per_batch):
    m, d = x2.shape
    gw = d // 2
    tiles_per_batch = rows_per_batch // tm
    half = pl.BlockSpec((tm, gw), lambda i: (i, 0))
    full = pl.BlockSpec((tm, d), lambda i: (i, 0))
    vec = lambda n: pl.BlockSpec((1, n), lambda i: (0, 0))
    return pl.pallas_call(
        _outproj_kernel,
        grid=(m // tm,),
        in_specs=[half, half, half,
                  pl.BlockSpec((None, tm, gw), lambda i: (2, i, 0)),
                  full,
                  pl.BlockSpec((1,) + mod3.shape[1:], lambda i: (i // tiles_per_batch, 0, 0)),
                  vec(gw), vec(d),
                  pl.BlockSpec((d, d), lambda i: (0, 0))],
        out_specs=full,
        out_shape=jax.ShapeDtypeStruct((m, d), F32),
        compiler_params=_params(1),
        name="outproj",
    )(uconv, o_f, o_b, z2, x2, mod3, rnw, postw, w_out_bf)


def _mlp_kernel(x1_ref, mod_ref, prew_ref, postw_ref, wu_ref, wd_ref, o_ref, h_scr):
    k = pl.program_id(1)
    last = pl.num_programs(1) - 1

    def partial_sum(rs):
        hid = jnp.maximum(_dot(h_scr[rs, :], wu_ref[...]), 0.0)
        return _dot((hid * hid).astype(BF16), wd_ref[...])

    @pl.when(k == 0)
    def _():
        for rs in _row_tiles(h_scr.shape[0], MLP_SUB_M):
            h = _rms_norm(x1_ref[rs, :], prew_ref[...])
            h_scr[rs, :] = (h * (1.0 + mod_ref[0, 2:3, :]) + mod_ref[0, 1:2, :]).astype(BF16)
            o_ref[rs, :] = partial_sum(rs)

    @pl.when((k > 0) & (k < last))
    def _():
        for rs in _row_tiles(h_scr.shape[0], MLP_SUB_M):
            o_ref[rs, :] += partial_sum(rs)

    @pl.when(k == last)
    def _():
        for rs in _row_tiles(h_scr.shape[0], MLP_SUB_M):
            y = o_ref[rs, :] + partial_sum(rs)
            o_ref[rs, :] = x1_ref[rs, :] + mod_ref[0, 3:4, :] * _rms_norm(y, postw_ref[...])


def _mlp(x1, mod3, prew, postw, w_up_bf, w_down_bf, tm, tf, rows_per_batch):
    m, d = x1.shape
    ff = w_up_bf.shape[1]
    assert ff // tf >= 2, "the first and last hidden chunks take different branches"
    tiles_per_batch = rows_per_batch // tm
    vec = pl.BlockSpec((1, d), lambda i, k: (0, 0))
    sub = min(tm, MLP_SUB_M)
    vmem_estimate = (2 * 2 * tm * d * 4 + tm * d * 2 + 2 * 2 * d * tf * 2
                     + sub * (tf * 4 + tf * 2 + d * 4) + VMEM_COMPILER_ALLOWANCE)
    return pl.pallas_call(
        _mlp_kernel,
        grid=(m // tm, ff // tf),
        in_specs=[pl.BlockSpec((tm, d), lambda i, k: (i, 0)),
                  pl.BlockSpec((1,) + mod3.shape[1:], lambda i, k: (i // tiles_per_batch, 0, 0)),
                  vec, vec,
                  pl.BlockSpec((d, tf), lambda i, k: (0, k)),
                  pl.BlockSpec((tf, d), lambda i, k: (k, 0))],
        out_specs=pl.BlockSpec((tm, d), lambda i, k: (i, 0)),
        out_shape=jax.ShapeDtypeStruct((m, d), F32),
        scratch_shapes=[pltpu.VMEM((tm, d), BF16)],
        compiler_params=_params(2, max(vmem_estimate, VMEM_LIMIT_BYTES)),
        name="mlp",
    )(x1, mod3, prew, postw, w_up_bf, w_down_bf)


def _tile(n, target):
    t = min(n, target)
    assert n % t == 0, (n, t)
    return t


def kernel(x, c, ctx, c_ctx, w_ada, b_ada, mix_pre_w, mix_post_w, mlp_pre_w, mlp_post_w, w_in, conv_w, conv_b,
           conv_ln_w, conv_ln_b, rec_lb_logits, rec_norm_w, w_out, w_up, w_down):
    bsz, seq, d = x.shape
    assert w_in.shape[0] == 1, "single-layer block"
    gw = d // 2
    assert gw % REC_DK == 0 and seq % REC_CHUNK == 0 and ctx.shape[1] % SUBLANES == 0
    m = bsz * seq

    cc = jnp.concatenate([c, c_ctx[None, :], jnp.zeros((SUBLANES - bsz - 1, d), c.dtype)], axis=0)
    mod_a = _ada(cc, w_ada[0], b_ada, _tile(2 * d, 1024), 2 * d).reshape(SUBLANES, 2, d)

    lbl = rec_lb_logits.astype(F32)

    s0f, s0b = _ctx_states(ctx, mod_a, mix_pre_w, lbl, w_in[0], col0=2 * gw + 2 * gw,
                           heads_per_step=gw // REC_DK)

    x2 = x.reshape(m, d)
    tm = _tile(seq, 512)
    z2 = _inproj(x2, mod_a, mix_pre_w, lbl, w_in[0], _tile(seq, 1024), _tile(gw, 512), seq)
    z3 = z2.reshape(z2.shape[0], bsz, seq, gw)

    tl = _tile(seq, 512)
    uconv, mod_b, (w_out_bf, w_up_bf, w_down_bf) = _conv(
        z3, conv_w[0], conv_b, conv_ln_w, conv_ln_b, tl, _tile(tl, 64), [w_out[0], w_up[0], w_down[0]],
        cc, w_ada[0], b_ada, ada_col0=2 * d)
    mod_b = mod_b.reshape(SUBLANES, 4, d)
    o_f, o_b = _gla(z3, s0f, s0b, _tile(seq, 512), gw)

    x1 = _outproj(uconv.reshape(m, gw), o_f.reshape(m, gw), o_b.reshape(m, gw), z2, x2, mod_b,
                  rec_norm_w, mix_post_w, w_out_bf, _tile(seq, 512), seq)
    out = _mlp(x1, mod_b, mlp_pre_w, mlp_post_w, w_up_bf, w_down_bf, _tile(seq, 1024), _tile(w_up.shape[2], 1024), seq)
    return out.reshape(bsz, seq, d)
```
